```python
import math
import jax
import jax.numpy as jnp
from jax import lax
import numpy as np


D_MODEL = 2048
BATCH = 2
SEQ = 16384
DEPTH = 4
DEC_BATCH = 1
DEC_SEQ = 16384
PAST_LEN = 128

GRID_W = 64
PE_DIM = 256
D_FF = 5632
D_CONV = 1024
D_HYENA = 1024
HYENA_ORDER = 2
N_DIR = 2
HF_EMB = 33
HF_BANDS = (HF_EMB - 1) // 2
HF_HIDDEN = 64
HF_MIN_DECAY = -math.log(1e-2) / 1.5
HF_MAX_DECAY = -math.log(1e-2) / 0.3
HEAD_DIM = 128
N_Q_HEADS = 8
N_KV_HEADS = 2
GQA_GROUP = N_Q_HEADS // N_KV_HEADS
Q_BLOCK = 128
ROPE_THETA = 10000.0
RMS_EPS = 1e-6
N_BRANCH = 3
Q_W = N_Q_HEADS * HEAD_DIM
KV_W = N_KV_HEADS * HEAD_DIM
W_IN_COLS = 3 * D_CONV + 3 * D_HYENA + Q_W + 2 * KV_W

kernel_name = "hybrid_conv_hyena_gqa_encoder"


def _rms_norm(x, g):
    xf = x.astype(jnp.float32)
    y = xf * lax.rsqrt(jnp.mean(xf * xf, axis=-1, keepdims=True) + RMS_EPS)
    return (y * g.astype(jnp.float32)).astype(x.dtype)


def _swiglu(x, w_up, w_down):
    a, b = jnp.split(x @ w_up, 2, axis=-1)
    return (jax.nn.silu(a) * b) @ w_down


def _conv3(x, w):
    xp = jnp.pad(x, ((0, 0), (1, 1), (0, 0)))
    return xp[:, :-2] * w[0] + xp[:, 1:-1] * w[1] + xp[:, 2:] * w[2]


def _hyena_pos_features(L):
    f32 = jnp.float32
    t = jnp.linspace(0.0, 1.0, L, dtype=f32)[:, None]
    w = 2.0 * math.pi * jnp.arange(L, dtype=f32)[:, None] / L
    f = jnp.linspace(1e-4, HF_BANDS - 1, HF_BANDS, dtype=f32)[None, :]
    ang = w * f
    return jnp.concatenate([t, jnp.cos(ang), -jnp.sin(ang)], axis=-1)


def _hyena_filter_spectra(L, w1, b1, fr1, w2, b2, fr2, w3, decay):
    f32 = jnp.float32
    z = _hyena_pos_features(L)
    h = jnp.sin(fr1.astype(f32) * (z @ w1.astype(f32) + b1.astype(f32)))
    h = jnp.sin(fr2.astype(f32) * (h @ w2.astype(f32) + b2.astype(f32)))
    h = h @ w3.astype(f32)
    t = jnp.linspace(0.0, 1.0, L, dtype=f32)[:, None]
    h = h * jnp.exp(-t * jnp.abs(decay.astype(f32))[None, :])
    h = h.reshape(L, HYENA_ORDER, N_DIR, D_HYENA)
    h_fwd = h[:, :, 0]
    h_bwd = h[1:, :, 1]
    l1 = jnp.sum(jnp.abs(h_fwd), axis=0) + jnp.sum(jnp.abs(h_bwd), axis=0)
    k = jnp.concatenate([h_fwd, jnp.zeros((1, HYENA_ORDER, D_HYENA), f32), h_bwd[::-1]], axis=0) / l1
    return jnp.fft.rfft(k, axis=0)


def _fft_long_conv(z, k_f, skip):
    L = z.shape[1]
    zf32 = z.astype(jnp.float32)
    zf = jnp.fft.rfft(zf32, n=2 * L, axis=1)
    y = jnp.fft.irfft(zf * k_f[None], n=2 * L, axis=1)[:, :L]
    return (y + zf32 * skip.astype(jnp.float32)).astype(z.dtype)


def _axial_angles(L):
    f32 = jnp.float32
    rows_n = L // GRID_W
    row = jnp.repeat(jnp.arange(rows_n, dtype=f32), GRID_W)
    col = jnp.tile(jnp.arange(GRID_W, dtype=f32), rows_n)
    half = HEAD_DIM // 2
    inv = ROPE_THETA ** (-jnp.arange(0, half, 2, dtype=f32) / half)
    return row[:, None] * inv[None, :], col[:, None] * inv[None, :]


def _rope_half(x, ang):
    d = x.shape[-1] // 2
    c = jnp.cos(ang)[None, :, None, :]
    s = jnp.sin(ang)[None, :, None, :]
    x1, x2 = x[..., :d], x[..., d:]
    return jnp.concatenate([x1 * c - x2 * s, x1 * s + x2 * c], axis=-1)


def _axial_rope(x, ang_row, ang_col):
    xf = x.astype(jnp.float32)
    half = HEAD_DIM // 2
    out = jnp.concatenate([_rope_half(xf[..., :half], ang_row), _rope_half(xf[..., half:], ang_col)], axis=-1)
    return out.astype(x.dtype)


def _blocked_attention(q, k, v):
    B, L = q.shape[0], q.shape[1]
    nb = L // Q_BLOCK
    qb = q.reshape(B, nb, Q_BLOCK, N_KV_HEADS, GQA_GROUP, HEAD_DIM).transpose(1, 0, 2, 3, 4, 5)
    scale = HEAD_DIM ** -0.5

    def one_block(qblk):
        s = jnp.einsum('bqkgd,bskd->bkgqs', qblk, k, preferred_element_type=jnp.float32) * scale
        pr = jax.nn.softmax(s, axis=-1)
        return jnp.einsum('bkgqs,bskd->bqkgd', pr.astype(v.dtype), v)

    o = lax.map(one_block, qb)
    return o.transpose(1, 0, 2, 3, 4, 5).reshape(B, L, Q_W)


def _layer(x, pe, i, P, ang_row, ang_col):
    B, L = x.shape[0], x.shape[1]
    h = x + 0.5 * _swiglu(_rms_norm(x, P['n_ffn1'][i]), P['w_ffn1_up'][i], P['w_ffn1_down'][i])
    u = _rms_norm(h, P['n_mix'][i])
    proj = u @ P['w_in'][i]
    cols = [D_CONV, D_CONV, D_CONV, 3 * D_HYENA, Q_W, KV_W, KV_W]
    idx = [int(c) for c in np.cumsum(cols)[:-1]]
    b_a, c_a, xin_a, hy, q, k, v = jnp.split(proj, idx, axis=-1)
    ya = b_a * _conv3(c_a * xin_a, P['conv_a_w'][i])
    hs = _conv3(hy, P['hy_short_w'][i])
    hv, hx1, hx2 = jnp.split(hs, 3, axis=-1)
    k_f = _hyena_filter_spectra(L, P['hf_w1'][i], P['hf_b1'][i], P['hf_freq1'][i], P['hf_w2'][i],
                                P['hf_b2'][i], P['hf_freq2'][i], P['hf_w3'][i], P['hf_decay'][i])
    zb = hv
    for o, gate in enumerate((hx1, hx2)):
        zb = gate * _fft_long_conv(zb, k_f[:, o], P['hf_skip'][i, o])
    qh = _rms_norm(q.reshape(B, L, N_Q_HEADS, HEAD_DIM), P['q_norm'][i])
    kh = _rms_norm(k.reshape(B, L, N_KV_HEADS, HEAD_DIM), P['k_norm'][i])
    vh = v.reshape(B, L, N_KV_HEADS, HEAD_DIM)
    qh = _axial_rope(qh, ang_row, ang_col)
    kh = _axial_rope(kh, ang_row, ang_col)
    yc = _blocked_attention(qh, kh, vh)
    ga, gb, gc = jnp.split(jax.nn.sigmoid(u @ P['w_gate'][i]), N_BRANCH, axis=-1)
    m = ga * (ya @ P['w_pa'][i]) + gb * (zb @ P['w_pb'][i]) + gc * (yc @ P['w_pc'][i])
    h = h + m @ P['w_o'][i]
    h = h + 0.5 * _swiglu(_rms_norm(h, P['n_ffn2'][i]), P['w_ffn2_up'][i], P['w_ffn2_down'][i])
    g = jax.nn.sigmoid(_rms_norm(h, P['n_pe'][i]) @ P['w_pe_gate'][i])
    return h + g * (pe @ P['w_pe_proj'][i])


def _trunk(x, pe, P):
    L = x.shape[1]
    ang_row, ang_col = _axial_angles(L)
    h = x
    for i in range(DEPTH):
        h = _layer(h, pe[i], i, P, ang_row, ang_col)
    return _rms_norm(h, P['final_norm'])


def _nrm(k, shape, scale):
    return jax.random.normal(k, shape, jnp.float32) * scale


def setup_inputs(seed: int = 0) -> dict:
    key = jax.random.key(seed)
    ks = iter(jax.random.split(key, 40))
    D = D_MODEL
    gain = lambda k, n: 1.0 + _nrm(k, (DEPTH, n), 0.02)
    return {
        'x_prompt': _nrm(next(ks), (BATCH, SEQ, D), 1.0),
        'x_sample': _nrm(next(ks), (DEC_BATCH, DEC_SEQ, D), 1.0),
        'p_prompt': _nrm(next(ks), (DEPTH, BATCH, SEQ, PE_DIM), 1.0),
        'p_sample': _nrm(next(ks), (DEPTH, DEC_BATCH, DEC_SEQ, PE_DIM), 1.0),
        'n_ffn1': gain(next(ks), D),
        'w_ffn1_up': _nrm(next(ks), (DEPTH, D, 2 * D_FF), D ** -0.5),
        'w_ffn1_down': _nrm(next(ks), (DEPTH, D_FF, D), D_FF ** -0.5),
        'n_mix': gain(next(ks), D),
        'w_in': _nrm(next(ks), (DEPTH, D, W_IN_COLS), D ** -0.5),
        'w_gate': _nrm(next(ks), (DEPTH, D, N_BRANCH * D), D ** -0.5),
        'conv_a_w': _nrm(next(ks), (DEPTH, 3, D_CONV), 3 ** -0.5),
        'hy_short_w': _nrm(next(ks), (DEPTH, 3, 3 * D_HYENA), 3 ** -0.5),
        'hf_w1': _nrm(next(ks), (DEPTH, HF_EMB, HF_HIDDEN), HF_EMB ** -0.5),
        'hf_b1': _nrm(next(ks), (DEPTH, HF_HIDDEN), 0.1),
        'hf_freq1': gain(next(ks), HF_HIDDEN),
        'hf_w2': _nrm(next(ks), (DEPTH, HF_HIDDEN, HF_HIDDEN), HF_HIDDEN ** -0.5),
        'hf_b2': _nrm(next(ks), (DEPTH, HF_HIDDEN), 0.1),
        'hf_freq2': gain(next(ks), HF_HIDDEN),
        'hf_w3': _nrm(next(ks), (DEPTH, HF_HIDDEN, HYENA_ORDER * N_DIR * D_HYENA), HF_HIDDEN ** -0.5),
        'hf_decay': jax.random.uniform(next(ks), (DEPTH, HYENA_ORDER * N_DIR * D_HYENA), jnp.float32,
                                       minval=HF_MIN_DECAY, maxval=HF_MAX_DECAY),
        'hf_skip': _nrm(next(ks), (DEPTH, HYENA_ORDER, D_HYENA), 1.0),
        'q_norm': gain(next(ks), HEAD_DIM),
        'k_norm': gain(next(ks), HEAD_DIM),
        'w_pa': _nrm(next(ks), (DEPTH, D_CONV, D), D_CONV ** -0.5),
        'w_pb': _nrm(next(ks), (DEPTH, D_HYENA, D), D_HYENA ** -0.5),
        'w_pc': _nrm(next(ks), (DEPTH, Q_W, D), Q_W ** -0.5),
        'w_o': _nrm(next(ks), (DEPTH, D, D), D ** -0.5),
        'n_ffn2': gain(next(ks), D),
        'w_ffn2_up': _nrm(next(ks), (DEPTH, D, 2 * D_FF), D ** -0.5),
        'w_ffn2_down': _nrm(next(ks), (DEPTH, D_FF, D), D_FF ** -0.5),
        'n_pe': gain(next(ks), D),
        'w_pe_gate': _nrm(next(ks), (DEPTH, D, D), D ** -0.5),
        'w_pe_proj': _nrm(next(ks), (DEPTH, PE_DIM, D), PE_DIM ** -0.5),
        'final_norm': 1.0 + _nrm(next(ks), (D,), 0.02),
    }


def reference(x_prompt, x_sample, p_prompt, p_sample, n_ffn1, w_ffn1_up, w_ffn1_down, n_mix, w_in,
              w_gate, conv_a_w, hy_short_w, hf_w1, hf_b1, hf_freq1, hf_w2, hf_b2, hf_freq2, hf_w3,
              hf_decay, hf_skip, q_norm, k_norm, w_pa, w_pb, w_pc, w_o, n_ffn2, w_ffn2_up, w_ffn2_down,
              n_pe, w_pe_gate, w_pe_proj, final_norm):
    P = dict(n_ffn1=n_ffn1, w_ffn1_up=w_ffn1_up, w_ffn1_down=w_ffn1_down, n_mix=n_mix, w_in=w_in,
             w_gate=w_gate, conv_a_w=conv_a_w, hy_short_w=hy_short_w, hf_w1=hf_w1, hf_b1=hf_b1,
             hf_freq1=hf_freq1, hf_w2=hf_w2, hf_b2=hf_b2, hf_freq2=hf_freq2, hf_w3=hf_w3,
             hf_decay=hf_decay, hf_skip=hf_skip, q_norm=q_norm, k_norm=k_norm, w_pa=w_pa, w_pb=w_pb,
             w_pc=w_pc, w_o=w_o, n_ffn2=n_ffn2, w_ffn2_up=w_ffn2_up, w_ffn2_down=w_ffn2_down,
             n_pe=n_pe, w_pe_gate=w_pe_gate, w_pe_proj=w_pe_proj, final_norm=final_norm)
    y_prompt = _trunk(x_prompt, p_prompt, P)
    y_sample = _trunk(x_sample, p_sample, P)
    return (y_prompt, y_sample)
```

```python
import functools
import math

import numpy as np
import jax
import jax.numpy as jnp
from jax import lax
from jax.experimental import pallas as pl
from jax.experimental.pallas import tpu as pltpu

F32 = jnp.float32
BF16 = jnp.bfloat16

RMS_EPS = 1e-6
ROPE_THETA = 10000.0
GRID_W = 64
LANES = 128
BF16_SUBLANES = 16
FFT_N1 = 128
VMEM_LIMIT_BYTES = 52 * 1024 * 1024

TILE_M = 1024
TILE_M_DOWN = 512
TILE_N = 512
TILE_L = 1024
TILE_C = 512
TILE_Q = 512
TILE_K = 512
TILE_FFT_COLS = 8192
TILE_K2 = 8
TILE_FFT_C = 512
TILE_HF = 256


def _tile(default, dim):
    t = min(default, dim)
    assert dim % t == 0, (default, dim)
    return t


def _params(*sem):
    return pltpu.CompilerParams(dimension_semantics=sem, vmem_limit_bytes=VMEM_LIMIT_BYTES)


def _rms_rows(x, g):
    ms = jnp.mean(x * x, axis=-1, keepdims=True)
    return x * lax.rsqrt(ms + RMS_EPS) * g


def _norm_mm_kernel(x_ref, g_ref, w_ref, o_ref, xn_ref, *, act):
    @pl.when(pl.program_id(1) == 0)
    def _():
        xn_ref[...] = _rms_rows(x_ref[...], g_ref[...]).astype(BF16)

    y = jnp.dot(xn_ref[...], w_ref[...], preferred_element_type=F32)
    if act == "sigmoid":
        y = jax.nn.sigmoid(y)
    o_ref[...] = y.astype(o_ref.dtype)


def norm_matmul(x, g, w, act=None):
    T, D = x.shape
    N = w.shape[1]
    tm, tn = _tile(TILE_M, T), _tile(TILE_N, N)
    return pl.pallas_call(
        functools.partial(_norm_mm_kernel, act=act),
        out_shape=jax.ShapeDtypeStruct((T, N), BF16),
        grid=(T // tm, N // tn),
        in_specs=[
            pl.BlockSpec((tm, D), lambda i, j: (i, 0)),
            pl.BlockSpec((1, D), lambda i, j: (0, 0)),
            pl.BlockSpec((D, tn), lambda i, j: (0, j)),
        ],
        out_specs=pl.BlockSpec((tm, tn), lambda i, j: (i, j)),
        scratch_shapes=[pltpu.VMEM((tm, D), BF16)],
        compiler_params=_params("parallel", "arbitrary"),
        name="norm_matmul",
    )(x, g.reshape(1, D), w)


def _norm_swiglu_kernel(x_ref, g_ref, wa_ref, wb_ref, o_ref, xn_ref):
    @pl.when(pl.program_id(1) == 0)
    def _():
        xn_ref[...] = _rms_rows(x_ref[...], g_ref[...]).astype(BF16)

    xn = xn_ref[...]
    a = jnp.dot(xn, wa_ref[...], preferred_element_type=F32)
    b = jnp.dot(xn, wb_ref[...], preferred_element_type=F32)
    o_ref[...] = (a * jax.nn.sigmoid(a) * b).astype(o_ref.dtype)


def norm_swiglu_up(x, g, w_up):
    T, D = x.shape
    F = w_up.shape[1] // 2
    tm, tn = _tile(TILE_M, T), _tile(TILE_N, F)
    nb = F // tn
    return pl.pallas_call(
        _norm_swiglu_kernel,
        out_shape=jax.ShapeDtypeStruct((T, F), BF16),
        grid=(T // tm, nb),
        in_specs=[
            pl.BlockSpec((tm, D), lambda i, j: (i, 0)),
            pl.BlockSpec((1, D), lambda i, j: (0, 0)),
            pl.BlockSpec((D, tn), lambda i, j: (0, j)),
            pl.BlockSpec((D, tn), lambda i, j: (0, nb + j)),
        ],
        out_specs=pl.BlockSpec((tm, tn), lambda i, j: (i, j)),
        scratch_shapes=[pltpu.VMEM((tm, D), BF16)],
        compiler_params=_params("parallel", "arbitrary"),
        name="norm_swiglu_up",
    )(x, g.reshape(1, D), w_up, w_up)


def _mm_residual_kernel(a_ref, w_ref, r_ref, o_ref, *, scale):
    y = jnp.dot(a_ref[...], w_ref[...], preferred_element_type=F32)
    o_ref[...] = r_ref[...] + scale * y


def matmul_residual(a, w, res, scale):
    T, K = a.shape
    N = w.shape[1]
    tm, tn = _tile(TILE_M_DOWN, T), _tile(TILE_N, N)
    return pl.pallas_call(
        functools.partial(_mm_residual_kernel, scale=scale),
        out_shape=jax.ShapeDtypeStruct((T, N), F32),
        grid=(T // tm, N // tn),
        in_specs=[
            pl.BlockSpec((tm, K), lambda i, j: (i, 0)),
            pl.BlockSpec((K, tn), lambda i, j: (0, j)),
            pl.BlockSpec((tm, tn), lambda i, j: (i, j)),
        ],
        out_specs=pl.BlockSpec((tm, tn), lambda i, j: (i, j)),
        compiler_params=_params("parallel", "arbitrary"),
        name="matmul_residual",
    )(a, w, res)


def _merge_kernel(ya_ref, zb_ref, yc_ref, ga_ref, gb_ref, gc_ref, wa_ref, wb_ref, wc_ref, o_ref):
    pa = jnp.dot(ya_ref[...], wa_ref[...], preferred_element_type=F32)
    pb = jnp.dot(zb_ref[...], wb_ref[...], preferred_element_type=F32)
    pc = jnp.dot(yc_ref[...], wc_ref[...], preferred_element_type=F32)
    m = ga_ref[...].astype(F32) * pa + gb_ref[...].astype(F32) * pb + gc_ref[...].astype(F32) * pc
    o_ref[...] = m.astype(o_ref.dtype)


def gated_merge(ya, zb, yc, gates, w_pa, w_pb, w_pc):
    T = ya.shape[0]
    D = w_pa.shape[1]
    tm, tn = _tile(TILE_M, T), _tile(TILE_N, D)
    nb = D // tn
    row = lambda a: pl.BlockSpec((tm, a.shape[1]), lambda i, j: (i, 0))
    col = lambda w: pl.BlockSpec((w.shape[0], tn), lambda i, j: (0, j))
    gate = lambda b: pl.BlockSpec((tm, tn), lambda i, j: (i, b * nb + j))
    return pl.pallas_call(
        _merge_kernel,
        out_shape=jax.ShapeDtypeStruct((T, D), BF16),
        grid=(T // tm, nb),
        in_specs=[row(ya), row(zb), row(yc), gate(0), gate(1), gate(2), col(w_pa), col(w_pb), col(w_pc)],
        out_specs=pl.BlockSpec((tm, tn), lambda i, j: (i, j)),
        compiler_params=_params("parallel", "arbitrary"),
        name="gated_merge",
    )(ya, zb, yc, gates, gates, gates, w_pa, w_pb, w_pc)


def _pe_kernel(x_ref, g_ref, wg_ref, pe_ref, wp_ref, r_ref, o_ref, xn_ref):
    @pl.when(pl.program_id(1) == 0)
    def _():
        xn_ref[...] = _rms_rows(x_ref[...], g_ref[...]).astype(BF16)

    gate = jax.nn.sigmoid(jnp.dot(xn_ref[...], wg_ref[...], preferred_element_type=F32))
    emb = jnp.dot(pe_ref[...], wp_ref[...], preferred_element_type=F32)
    o_ref[...] = r_ref[...] + gate * emb


def pe_inject(h, g, w_gate, pe, w_proj):
    T, D = h.shape
    P = pe.shape[1]
    tm, tn = _tile(TILE_M, T), _tile(TILE_N, D)
    return pl.pallas_call(
        _pe_kernel,
        out_shape=jax.ShapeDtypeStruct((T, D), F32),
        grid=(T // tm, D // tn),
        in_specs=[
            pl.BlockSpec((tm, D), lambda i, j: (i, 0)),
            pl.BlockSpec((1, D), lambda i, j: (0, 0)),
            pl.BlockSpec((D, tn), lambda i, j: (0, j)),
            pl.BlockSpec((tm, P), lambda i, j: (i, 0)),
            pl.BlockSpec((P, tn), lambda i, j: (0, j)),
            pl.BlockSpec((tm, tn), lambda i, j: (i, j)),
        ],
        out_specs=pl.BlockSpec((tm, tn), lambda i, j: (i, j)),
        scratch_shapes=[pltpu.VMEM((tm, D), BF16)],
        compiler_params=_params("parallel", "arbitrary"),
        name="pe_inject",
    )(h, g.reshape(1, D), w_gate, pe, w_proj, h)


def _rmsnorm_kernel(x_ref, g_ref, o_ref):
    o_ref[...] = _rms_rows(x_ref[...], g_ref[...])


def rms_norm(x, g):
    T, D = x.shape
    tm = _tile(TILE_M_DOWN, T)
    return pl.pallas_call(
        _rmsnorm_kernel,
        out_shape=jax.ShapeDtypeStruct((T, D), F32),
        grid=(T // tm,),
        in_specs=[pl.BlockSpec((tm, D), lambda i: (i, 0)), pl.BlockSpec((1, D), lambda i: (0, 0))],
        out_specs=pl.BlockSpec((tm, D), lambda i: (i, 0)),
        compiler_params=_params("parallel"),
        name="rms_norm",
    )(x, g.reshape(1, D))


def _conv3_rows(x, prev_row, next_row, w):
    tl = x.shape[0]
    rows = lax.broadcasted_iota(jnp.int32, x.shape, 0)
    xm = jnp.where(rows == 0, prev_row, pltpu.roll(x, 1, 0))
    xp = jnp.where(rows == tl - 1, next_row, pltpu.roll(x, tl - 1, 0))
    return xm * w[0:1] + x * w[1:2] + xp * w[2:3]


def _halo_specs(tl, tc, n_l, col_block):
    hb = tl // BF16_SUBLANES
    n_h = n_l * hb
    main = pl.BlockSpec((1, tl, tc), lambda b, l, c: (b, l, col_block(c)))
    prev = pl.BlockSpec((1, BF16_SUBLANES, tc), lambda b, l, c: (b, jnp.maximum(l * hb - 1, 0), col_block(c)))
    nxt = pl.BlockSpec((1, BF16_SUBLANES, tc), lambda b, l, c: (b, jnp.minimum((l + 1) * hb, n_h - 1), col_block(c)))
    return main, prev, nxt


def _edge_rows(prev_ref, next_ref):
    l = pl.program_id(1)
    prev = prev_ref[0].astype(F32)[BF16_SUBLANES - 1:BF16_SUBLANES]
    nxt = next_ref[0].astype(F32)[0:1]
    prev = jnp.where(l == 0, 0.0, prev)
    nxt = jnp.where(l == pl.num_programs(1) - 1, 0.0, nxt)
    return prev, nxt


def _conv_a_kernel(b_ref, c_ref, cp_ref, cn_ref, x_ref, xp_ref, xn_ref, w_ref, o_ref):
    cprev, cnext = _edge_rows(cp_ref, cn_ref)
    xprev, xnext = _edge_rows(xp_ref, xn_ref)
    p = c_ref[0].astype(F32) * x_ref[0].astype(F32)
    y = _conv3_rows(p, cprev * xprev, cnext * xnext, w_ref[...])
    o_ref[0] = (b_ref[0].astype(F32) * y).astype(o_ref.dtype)


def conv_a(proj, w, dc):
    B, L, _ = proj.shape
    tl, tc = _tile(TILE_L, L), _tile(TILE_C, dc)
    nc, nl = dc // tc, L // tl
    b_spec = pl.BlockSpec((1, tl, tc), lambda b, l, c: (b, l, c))
    c_main, c_prev, c_next = _halo_specs(tl, tc, nl, lambda c: nc + c)
    x_main, x_prev, x_next = _halo_specs(tl, tc, nl, lambda c: 2 * nc + c)
    return pl.pallas_call(
        _conv_a_kernel,
        out_shape=jax.ShapeDtypeStruct((B, L, dc), BF16),
        grid=(B, nl, nc),
        in_specs=[b_spec, c_main, c_prev, c_next, x_main, x_prev, x_next,
                  pl.BlockSpec((3, tc), lambda b, l, c: (0, c))],
        out_specs=pl.BlockSpec((1, tl, tc), lambda b, l, c: (b, l, c)),
        compiler_params=_params("parallel", "arbitrary", "arbitrary"),
        name="conv_a",
    )(proj, proj, proj, proj, proj, proj, proj, w)


def _hy_short_kernel(*refs):
    ins, w_refs, outs = refs[:9], refs[9:12], refs[12:]
    for s in range(3):
        x_ref, p_ref, n_ref = ins[3 * s:3 * s + 3]
        prev, nxt = _edge_rows(p_ref, n_ref)
        y = _conv3_rows(x_ref[0].astype(F32), prev, nxt, w_refs[s][...])
        outs[s][0] = y.astype(outs[s].dtype)


def hyena_short_conv(proj, w, col0, dh):
    B, L, _ = proj.shape
    tl, tc = _tile(TILE_L, L), _tile(TILE_C, dh)
    nc, nl = dh // tc, L // tl
    base = col0 // tc
    in_specs, w_specs = [], []
    for s in range(3):
        in_specs += list(_halo_specs(tl, tc, nl, lambda c, s=s: base + s * nc + c))
        w_specs.append(pl.BlockSpec((3, tc), lambda b, l, c, s=s: (0, s * nc + c)))
    out_spec = pl.BlockSpec((1, tl, tc), lambda b, l, c: (b, l, c))
    out = jax.ShapeDtypeStruct((B, L, dh), BF16)
    return pl.pallas_call(
        _hy_short_kernel,
        out_shape=(out, out, out),
        grid=(B, nl, nc),
        in_specs=in_specs + w_specs,
        out_specs=(out_spec, out_spec, out_spec),
        compiler_params=_params("parallel", "arbitrary", "arbitrary"),
        name="hyena_short_conv",
    )(*([proj] * 9), w, w, w)


def _qk_prep_kernel(x_ref, qg_ref, kg_ref, cos_ref, sin_ref, o_ref, *, n_q, scale):
    head = pl.program_id(2)
    is_q = head < n_q
    x = x_ref[0].astype(F32)
    g = jnp.where(is_q, qg_ref[...], kg_ref[...])
    xn = _rms_rows(x, g)
    lane = lax.broadcasted_iota(jnp.int32, xn.shape, 1)
    quarter = xn.shape[1] // 4
    swapped = jnp.where((lane % (2 * quarter)) < quarter,
                        pltpu.roll(xn, 3 * quarter, 1), pltpu.roll(xn, quarter, 1))
    y = xn * cos_ref[...] + swapped * sin_ref[...]
    y = y * jnp.where(is_q, scale, 1.0)
    o_ref[0, 0] = y.astype(o_ref.dtype)


def qk_prep(proj, q_gain, k_gain, cos_t, sin_t, col0, n_q, n_kv, hd):
    B, L, _ = proj.shape
    tl = _tile(TILE_L, L)
    base = col0 // hd
    return pl.pallas_call(
        functools.partial(_qk_prep_kernel, n_q=n_q, scale=hd ** -0.5),
        out_shape=jax.ShapeDtypeStruct((B, n_q + n_kv, L, hd), BF16),
        grid=(B, L // tl, n_q + n_kv),
        in_specs=[
            pl.BlockSpec((1, tl, hd), lambda b, l, h: (b, l, base + h)),
            pl.BlockSpec((1, hd), lambda b, l, h: (0, 0)),
            pl.BlockSpec((1, hd), lambda b, l, h: (0, 0)),
            pl.BlockSpec((tl, hd), lambda b, l, h: (l, 0)),
            pl.BlockSpec((tl, hd), lambda b, l, h: (l, 0)),
        ],
        out_specs=pl.BlockSpec((1, 1, tl, hd), lambda b, l, h: (b, h, l, 0)),
        compiler_params=_params("parallel", "arbitrary", "arbitrary"),
        name="qk_prep",
    )(proj, q_gain.reshape(1, hd), k_gain.reshape(1, hd), cos_t, sin_t)


def _flash_kernel(q_ref, k_ref, v_ref, o_ref, m_ref, l_ref, acc_ref):
    ki = pl.program_id(3)
    group, tq, hd = q_ref.shape[1:]

    @pl.when(ki == 0)
    def _():
        m_ref[...] = jnp.full(m_ref.shape, -jnp.inf, F32)
        l_ref[...] = jnp.zeros(l_ref.shape, F32)
        acc_ref[...] = jnp.zeros(acc_ref.shape, F32)

    q = q_ref[0].reshape(group * tq, hd)
    s = lax.dot_general(q, k_ref[0, 0], (((1,), (1,)), ((), ())), preferred_element_type=F32)
    m_prev = m_ref[...]
    m_new = jnp.maximum(m_prev, jnp.max(s, axis=1, keepdims=True))
    alpha = jnp.exp(m_prev - m_new)
    p = jnp.exp(s - m_new)
    l_ref[...] = alpha * l_ref[...] + jnp.sum(p, axis=1, keepdims=True)
    acc_ref[...] = alpha * acc_ref[...] + jnp.dot(p.astype(BF16), v_ref[0], preferred_element_type=F32)
    m_ref[...] = m_new

    @pl.when(ki == pl.num_programs(3) - 1)
    def _():
        out = acc_ref[...] / l_ref[...]
        for g in range(group):
            o_ref[0, :, g * hd:(g + 1) * hd] = out[g * tq:(g + 1) * tq].astype(o_ref.dtype)


def flash_attention(qk, proj, v_col0, n_q, n_kv, hd):
    B, _, L, _ = qk.shape
    group = n_q // n_kv
    tq, tk = _tile(TILE_Q, L), _tile(TILE_K, L)
    v_base = v_col0 // hd
    return pl.pallas_call(
        _flash_kernel,
        out_shape=jax.ShapeDtypeStruct((B, L, n_q * hd), BF16),
        grid=(B, n_kv, L // tq, L // tk),
        in_specs=[
            pl.BlockSpec((1, group, tq, hd), lambda b, h, i, j: (b, h, i, 0)),
            pl.BlockSpec((1, 1, tk, hd), lambda b, h, i, j: (b, n_q + h, j, 0)),
            pl.BlockSpec((1, tk, hd), lambda b, h, i, j: (b, j, v_base + h)),
        ],
        out_specs=pl.BlockSpec((1, tq, group * hd), lambda b, h, i, j: (b, i, h)),
        scratch_shapes=[
            pltpu.VMEM((group * tq, 1), F32),
            pltpu.VMEM((group * tq, 1), F32),
            pltpu.VMEM((group * tq, hd), F32),
        ],
        compiler_params=_params("parallel", "parallel", "parallel", "arbitrary"),
        name="flash_attention",
    )(qk, qk, proj)


def _hyena_filter_kernel(z_ref, w1_ref, b1_ref, f1_ref, w2_ref, b2_ref, f2_ref, w3_ref, dec_ref, bwd_ref,
                         h_ref, l1_ref):
    i = pl.program_id(0)
    hi = lax.Precision.HIGHEST
    z = z_ref[...]
    h = jnp.sin(f1_ref[...] * (jnp.dot(z, w1_ref[...], precision=hi, preferred_element_type=F32) + b1_ref[...]))
    h = jnp.sin(f2_ref[...] * (jnp.dot(h, w2_ref[...], precision=hi, preferred_element_type=F32) + b2_ref[...]))
    h = jnp.dot(h, w3_ref[...], precision=hi, preferred_element_type=F32)
    h = h * jnp.exp(-z[:, 0:1] * jnp.abs(dec_ref[...]))
    rows = lax.broadcasted_iota(jnp.int32, h.shape, 0)
    h = jnp.where((rows == 0) & (i == 0) & (bwd_ref[...] > 0.5), 0.0, h)

    @pl.when(i == 0)
    def _():
        l1_ref[...] = jnp.zeros(l1_ref.shape, F32)

    l1_ref[...] += jnp.sum(jnp.abs(h), axis=0, keepdims=True)
    h_ref[...] = h.astype(h_ref.dtype)


def hyena_filter_taps(zfeat, w1, b1, f1, w2, b2, f2, w3, decay, bwd_mask):
    L, E = zfeat.shape
    H = w2.shape[0]
    C4 = w3.shape[1]
    tt = _tile(TILE_HF, L)
    full = lambda a: pl.BlockSpec(a.shape, lambda i: (0, 0))
    args = (zfeat, w1, b1.reshape(1, H), f1.reshape(1, H), w2, b2.reshape(1, H), f2.reshape(1, H), w3,
            decay.reshape(1, C4), bwd_mask.reshape(1, C4))
    return pl.pallas_call(
        _hyena_filter_kernel,
        out_shape=(jax.ShapeDtypeStruct((L, C4), BF16), jax.ShapeDtypeStruct((1, C4), F32)),
        grid=(L // tt,),
        in_specs=[pl.BlockSpec((tt, E), lambda i: (i, 0))] + [full(a) for a in args[1:]],
        out_specs=(pl.BlockSpec((tt, C4), lambda i: (i, 0)), pl.BlockSpec((1, C4), lambda i: (0, 0))),
        compiler_params=_params("arbitrary"),
        name="hyena_filter_taps",
    )(*args)


def _dft_stage1_kernel(w_ref, z_ref, o_ref):
    o_ref[0] = jnp.dot(w_ref[...], z_ref[0], preferred_element_type=F32).astype(o_ref.dtype)


def dft_stage1(w1, z2):
    B, K, cols = z2.shape
    M = w1.shape[0]
    tn = _tile(TILE_FFT_COLS, cols)
    return pl.pallas_call(
        _dft_stage1_kernel,
        out_shape=jax.ShapeDtypeStruct((B, M, cols), BF16),
        grid=(B, cols // tn),
        in_specs=[pl.BlockSpec((M, K), lambda b, j: (0, 0)), pl.BlockSpec((1, K, tn), lambda b, j: (b, 0, j))],
        out_specs=pl.BlockSpec((1, M, tn), lambda b, j: (b, 0, j)),
        compiler_params=_params("parallel", "arbitrary"),
        name="dft_stage1",
    )(w1, z2)


def _filter_spectrum_kernel(g_ref, af_ref, ab_ref, lf_ref, lb_ref, o_ref):
    n1 = af_ref.shape[3]
    inv_l1 = 1.0 / (lf_ref[...] + lb_ref[...])
    for kk in range(af_ref.shape[1]):
        g = g_ref[kk]
        xf = jnp.dot(g, af_ref[0, kk].reshape(2 * n1, -1), preferred_element_type=F32)
        xb = jnp.dot(g, ab_ref[0, kk].reshape(2 * n1, -1), preferred_element_type=F32)
        o_ref[0, kk, 0] = (xf[:n1] + xb[:n1]) * inv_l1
        o_ref[0, kk, 1] = (xf[n1:] - xb[n1:]) * inv_l1


def filter_spectrum(g_tab, a5, l1, n_order, dh):
    _, k2p, _, n1, _ = a5.shape
    kb, cb = _tile(TILE_K2, k2p), _tile(TILE_FFT_C, dh)
    nc = dh // cb
    a_spec = lambda d: pl.BlockSpec((1, kb, 2, n1, cb), lambda k, o, c: (0, k, 0, 0, (2 * o + d) * nc + c))
    l_spec = lambda d: pl.BlockSpec((1, cb), lambda k, o, c: (0, (2 * o + d) * nc + c))
    return pl.pallas_call(
        _filter_spectrum_kernel,
        out_shape=jax.ShapeDtypeStruct((n_order, k2p, 2, n1, dh), F32),
        grid=(k2p // kb, n_order, nc),
        in_specs=[pl.BlockSpec((kb, 2 * n1, 2 * n1), lambda k, o, c: (k, 0, 0)),
                  a_spec(0), a_spec(1), l_spec(0), l_spec(1)],
        out_specs=pl.BlockSpec((1, kb, 2, n1, cb), lambda k, o, c: (o, k, 0, 0, c)),
        compiler_params=_params("parallel", "arbitrary", "arbitrary"),
        name="filter_spectrum",
    )(g_tab, a5, a5, l1, l1)


def _dft_stage2_kernel(g_ref, gt_ref, a_ref, kf_ref, o_ref):
    n1 = a_ref.shape[3]
    for kk in range(a_ref.shape[1]):
        x = jnp.dot(g_ref[kk], a_ref[0, kk].reshape(2 * n1, -1), preferred_element_type=F32)
        xr, xi = x[:n1], x[n1:]
        kr, ki = kf_ref[0, kk, 0], kf_ref[0, kk, 1]
        p = jnp.concatenate([xr * kr - xi * ki, xr * ki + xi * kr], axis=0).astype(BF16)
        q = jnp.dot(gt_ref[kk], p, preferred_element_type=F32)
        o_ref[0, kk] = q.reshape(2, n1, -1).astype(o_ref.dtype)


def dft_stage2(g_tab, gt_tab, a5, kf, order):
    B, k2p, _, n1, C = a5.shape
    kb, cb = _tile(TILE_K2, k2p), _tile(TILE_FFT_C, C)
    g_spec = pl.BlockSpec((kb, 2 * n1, 2 * n1), lambda k, c, b: (k, 0, 0))
    return pl.pallas_call(
        _dft_stage2_kernel,
        out_shape=jax.ShapeDtypeStruct(a5.shape, BF16),
        grid=(k2p // kb, C // cb, B),
        in_specs=[g_spec, g_spec,
                  pl.BlockSpec((1, kb, 2, n1, cb), lambda k, c, b: (b, k, 0, 0, c)),
                  pl.BlockSpec((1, kb, 2, n1, cb), lambda k, c, b: (order, k, 0, 0, c))],
        out_specs=pl.BlockSpec((1, kb, 2, n1, cb), lambda k, c, b: (b, k, 0, 0, c)),
        compiler_params=_params("parallel", "arbitrary", "arbitrary"),
        name="dft_stage2",
    )(g_tab, gt_tab, a5, kf)


def _dft_stage3_kernel(w_ref, q_ref, z_ref, gate_ref, skip_ref, o_ref):
    y = jnp.dot(w_ref[...], q_ref[0], preferred_element_type=F32)
    z = z_ref[0].astype(F32)
    o_ref[0] = (gate_ref[0].astype(F32) * (y + z * skip_ref[...])).astype(o_ref.dtype)


def dft_stage3(wi, q2, z2, gate2, skip_cols):
    B, M, cols = q2.shape
    n2h = wi.shape[0]
    tn = _tile(TILE_FFT_COLS, cols)
    blk = pl.BlockSpec((1, n2h, tn), lambda b, j: (b, 0, j))
    return pl.pallas_call(
        _dft_stage3_kernel,
        out_shape=jax.ShapeDtypeStruct((B, n2h, cols), BF16),
        grid=(B, cols // tn),
        in_specs=[pl.BlockSpec((n2h, M), lambda b, j: (0, 0)), pl.BlockSpec((1, M, tn), lambda b, j: (b, 0, j)),
                  blk, blk, pl.BlockSpec((1, tn), lambda b, j: (0, j))],
        out_specs=blk,
        compiler_params=_params("parallel", "arbitrary"),
        name="dft_stage3",
    )(wi, q2, z2, gate2, skip_cols)


def _dft_tables(L):
    n1 = FFT_N1
    n2h = L // n1
    n2 = 2 * n2h
    n = 2 * L
    k2 = n2h + 1
    k2p = -(-k2 // 8) * 8
    kk = np.arange(k2)
    ang1 = 2.0 * np.pi * np.outer(kk, np.arange(n2h)) / n2
    w1 = np.zeros((k2p, 2, n2h))
    w1[:k2, 0], w1[:k2, 1] = np.cos(ang1), -np.sin(ang1)
    c = np.where((kk == 0) | (kk == n2h), 1.0, 2.0)
    wi = np.zeros((n2h, k2p, 2))
    wi[:, :k2, 0], wi[:, :k2, 1] = (c[:, None] * np.cos(ang1)).T / n, (-c[:, None] * np.sin(ang1)).T / n
    freq = kk[:, None, None] + n2 * np.arange(n1)[None, :, None]
    ang2 = 2.0 * np.pi * freq * np.arange(n1)[None, None, :] / n
    gr, gi = np.cos(ang2), -np.sin(ang2)
    g = np.zeros((k2p, 2 * n1, 2 * n1))
    g[:k2] = np.concatenate([np.concatenate([gr, -gi], axis=2), np.concatenate([gi, gr], axis=2)], axis=1)
    to = lambda a: jnp.asarray(a.astype(np.float32)).astype(BF16)
    return to(w1.reshape(2 * k2p, n2h)), to(wi.reshape(n2h, 2 * k2p)), to(g), to(g.transpose(0, 2, 1)), k2p


def _rope_tables(L, hd):
    quarter = hd // 4
    rows_n = L // GRID_W
    row = jnp.repeat(jnp.arange(rows_n, dtype=F32), GRID_W)
    col = jnp.tile(jnp.arange(GRID_W, dtype=F32), rows_n)
    half = hd // 2
    inv = ROPE_THETA ** (-jnp.arange(0, half, 2, dtype=F32) / half)
    ar, ac = row[:, None] * inv[None, :], col[:, None] * inv[None, :]
    cos_t = jnp.concatenate([jnp.cos(ar), jnp.cos(ar), jnp.cos(ac), jnp.cos(ac)], axis=-1)
    sin_t = jnp.concatenate([-jnp.sin(ar), jnp.sin(ar), -jnp.sin(ac), jnp.sin(ac)], axis=-1)
    assert cos_t.shape == (L, 4 * quarter)
    return cos_t, sin_t


def _hyena_pos_features(L, emb):
    bands = (emb - 1) // 2
    t = jnp.linspace(0.0, 1.0, L, dtype=F32)[:, None]
    w = 2.0 * math.pi * jnp.arange(L, dtype=F32)[:, None] / L
    f = jnp.linspace(1e-4, bands - 1, bands, dtype=F32)[None, :]
    ang = w * f
    return jnp.concatenate([t, jnp.cos(ang), -jnp.sin(ang)], axis=-1)


def kernel(x_prompt, x_sample, p_prompt, p_sample, n_ffn1, w_ffn1_up, w_ffn1_down, n_mix, w_in, w_gate, conv_a_w, hy_short_w, hf_w1, hf_b1, hf_freq1, hf_w2, hf_b2, hf_freq2, hf_w3, hf_decay, hf_skip, q_norm, k_norm, w_pa, w_pb, w_pc, w_o, n_ffn2, w_ffn2_up, w_ffn2_down, n_pe, w_pe_gate, w_pe_proj, final_norm):
    assert x_prompt.shape[1:] == x_sample.shape[1:]
    bp, L, D = x_prompt.shape
    B = bp + x_sample.shape[0]
    T = B * L
    depth = w_in.shape[0]
    dc = conv_a_w.shape[-1]
    n_order, dh = hf_skip.shape[1:]
    hd = q_norm.shape[-1]
    qw = w_pc.shape[1]
    kvw = (w_in.shape[-1] - 3 * dc - 3 * dh - qw) // 2
    n_q, n_kv = qw // hd, kvw // hd
    emb = hf_w1.shape[1]
    assert n_order == 2 and hf_w3.shape[-1] == 4 * dh and L % (FFT_N1 * 8) == 0 and L % GRID_W == 0

    h = jnp.concatenate([x_prompt, x_sample], axis=0).reshape(T, D)
    pe_all = jnp.concatenate([p_prompt, p_sample], axis=1).astype(BF16).reshape(depth, T, -1)

    w1_tab, wi_tab, g_tab, gt_tab, k2p = _dft_tables(L)
    cos_t, sin_t = _rope_tables(L, hd)
    emb_p = -(-emb // 8) * 8
    zfeat = jnp.pad(_hyena_pos_features(L, emb), ((0, 0), (0, emb_p - emb)))
    bwd_mask = jnp.tile(jnp.concatenate([jnp.zeros((dh,), F32), jnp.ones((dh,), F32)]), n_order)
    n2h = L // FFT_N1
    fft_cols = FFT_N1 * dh

    def long_conv(z, gate, kf, order, skip):
        a = dft_stage1(w1_tab, z.reshape(B, n2h, fft_cols))
        q = dft_stage2(g_tab, gt_tab, a.reshape(B, k2p, 2, FFT_N1, dh), kf, order)
        y = dft_stage3(wi_tab, q.reshape(B, 2 * k2p, fft_cols), z.reshape(B, n2h, fft_cols),
                       gate.reshape(B, n2h, fft_cols), jnp.tile(skip, FFT_N1).reshape(1, fft_cols))
        return y.reshape(B, L, dh)

    for i in range(depth):
        bf = lambda w: w[i].astype(BF16)
        act = norm_swiglu_up(h, n_ffn1[i], bf(w_ffn1_up))
        h = matmul_residual(act, bf(w_ffn1_down), h, 0.5)
        proj = norm_matmul(h, n_mix[i], bf(w_in)).reshape(B, L, -1)
        gates = norm_matmul(h, n_mix[i], bf(w_gate), act="sigmoid")
        ya = conv_a(proj, conv_a_w[i], dc)
        hv, hx1, hx2 = hyena_short_conv(proj, hy_short_w[i], 3 * dc, dh)
        w1p = jnp.pad(hf_w1[i], ((0, emb_p - emb), (0, 0)))
        taps, l1 = hyena_filter_taps(zfeat, w1p, hf_b1[i], hf_freq1[i], hf_w2[i], hf_b2[i], hf_freq2[i],
                                     hf_w3[i], hf_decay[i], bwd_mask)
        af = dft_stage1(w1_tab, taps.reshape(1, n2h, FFT_N1 * 4 * dh))
        kf = filter_spectrum(g_tab, af.reshape(1, k2p, 2, FFT_N1, 4 * dh), l1, n_order, dh)
        zb = long_conv(hv, hx1, kf, 0, hf_skip[i, 0])
        zb = long_conv(zb, hx2, kf, 1, hf_skip[i, 1])
        qk = qk_prep(proj, q_norm[i], k_norm[i], cos_t, sin_t, 3 * dc + 3 * dh, n_q, n_kv, hd)
        yc = flash_attention(qk, proj, 3 * dc + 3 * dh + qw + kvw, n_q, n_kv, hd)
        m = gated_merge(ya.reshape(T, dc), zb.reshape(T, dh), yc.reshape(T, qw), gates,
                        bf(w_pa), bf(w_pb), bf(w_pc))
        h = matmul_residual(m, bf(w_o), h, 1.0)
        act = norm_swiglu_up(h, n_ffn2[i], bf(w_ffn2_up))
        h = matmul_residual(act, bf(w_ffn2_down), h, 0.5)
        h = pe_inject(h, n_pe[i], bf(w_pe_gate), pe_all[i], bf(w_pe_proj))

    y = rms_norm(h, final_norm).reshape(B, L, D)
    return (y[:bp], y[bp:])
```

```python
import functools
import math

import numpy as np
import jax
import jax.numpy as jnp
from jax import lax
from jax.experimental import pallas as pl
from jax.experimental.pallas import tpu as pltpu

F32 = jnp.float32
BF16 = jnp.bfloat16

RMS_EPS = 1e-6
ROPE_THETA = 10000.0
GRID_W = 64
LANES = 128
BF16_SUBLANES = 16
FFT_N1 = 128
VMEM_LIMIT_BYTES = 52 * 1024 * 1024

TILE_M = 1024
TILE_M_DOWN = 512
TILE_N = 512
TILE_L = 1024
TILE_C = 512
TILE_Q = 256
TILE_K = 512
TILE_FFT_COLS = 8192
TILE_K2 = 8
TILE_FFT_C = 512
TILE_HF = 256


def _tile(default, dim):
    t = min(default, dim)
    assert dim % t == 0, (default, dim)
    return t


def _params(*sem):
    return pltpu.CompilerParams(dimension_semantics=sem, vmem_limit_bytes=VMEM_LIMIT_BYTES)


def _rms_rows(x, g):
    ms = jnp.mean(x * x, axis=-1, keepdims=True)
    return x * lax.rsqrt(ms + RMS_EPS) * g


def _norm_mm_kernel(x_ref, g_ref, w_ref, o_ref, xn_ref, *, act):
    @pl.when(pl.program_id(1) == 0)
    def _():
        xn_ref[...] = _rms_rows(x_ref[...], g_ref[...]).astype(BF16)

    y = jnp.dot(xn_ref[...], w_ref[...], preferred_element_type=F32)
    if act == "sigmoid":
        y = jax.nn.sigmoid(y)
    o_ref[...] = y.astype(o_ref.dtype)


def norm_matmul(x, g, w, act=None):
    T, D = x.shape
    N = w.shape[1]
    tm, tn = _tile(TILE_M, T), _tile(TILE_N, N)
    return pl.pallas_call(
        functools.partial(_norm_mm_kernel, act=act),
        out_shape=jax.ShapeDtypeStruct((T, N), BF16),
        grid=(T // tm, N // tn),
        in_specs=[
            pl.BlockSpec((tm, D), lambda i, j: (i, 0)),
            pl.BlockSpec((1, D), lambda i, j: (0, 0)),
            pl.BlockSpec((D, tn), lambda i, j: (0, j)),
        ],
        out_specs=pl.BlockSpec((tm, tn), lambda i, j: (i, j)),
        scratch_shapes=[pltpu.VMEM((tm, D), BF16)],
        compiler_params=_params("parallel", "arbitrary"),
        name="norm_matmul",
    )(x, g.reshape(1, D), w)


def _norm_swiglu_kernel(x_ref, g_ref, wa_ref, wb_ref, o_ref, xn_ref):
    @pl.when(pl.program_id(1) == 0)
    def _():
        xn_ref[...] = _rms_rows(x_ref[...], g_ref[...]).astype(BF16)

    xn = xn_ref[...]
    a = jnp.dot(xn, wa_ref[...], preferred_element_type=F32)
    b = jnp.dot(xn, wb_ref[...], preferred_element_type=F32)
    o_ref[...] = (a * jax.nn.sigmoid(a) * b).astype(o_ref.dtype)


def norm_swiglu_up(x, g, w_up):
    T, D = x.shape
    F = w_up.shape[1] // 2
    tm, tn = _tile(TILE_M, T), _tile(TILE_N, F)
    nb = F // tn
    return pl.pallas_call(
        _norm_swiglu_kernel,
        out_shape=jax.ShapeDtypeStruct((T, F), BF16),
        grid=(T // tm, nb),
        in_specs=[
            pl.BlockSpec((tm, D), lambda i, j: (i, 0)),
            pl.BlockSpec((1, D), lambda i, j: (0, 0)),
            pl.BlockSpec((D, tn), lambda i, j: (0, j)),
            pl.BlockSpec((D, tn), lambda i, j: (0, nb + j)),
        ],
        out_specs=pl.BlockSpec((tm, tn), lambda i, j: (i, j)),
        scratch_shapes=[pltpu.VMEM((tm, D), BF16)],
        compiler_params=_params("parallel", "arbitrary"),
        name="norm_swiglu_up",
    )(x, g.reshape(1, D), w_up, w_up)


def _mm_residual_kernel(a_ref, w_ref, r_ref, o_ref, *, scale):
    y = jnp.dot(a_ref[...], w_ref[...], preferred_element_type=F32)
    o_ref[...] = r_ref[...] + scale * y


def matmul_residual(a, w, res, scale):
    T, K = a.shape
    N = w.shape[1]
    tm, tn = _tile(TILE_M_DOWN, T), _tile(TILE_N, N)
    return pl.pallas_call(
        functools.partial(_mm_residual_kernel, scale=scale),
        out_shape=jax.ShapeDtypeStruct((T, N), F32),
        grid=(T // tm, N // tn),
        in_specs=[
            pl.BlockSpec((tm, K), lambda i, j: (i, 0)),
            pl.BlockSpec((K, tn), lambda i, j: (0, j)),
            pl.BlockSpec((tm, tn), lambda i, j: (i, j)),
        ],
        out_specs=pl.BlockSpec((tm, tn), lambda i, j: (i, j)),
        compiler_params=_params("parallel", "arbitrary"),
        name="matmul_residual",
    )(a, w, res)


def _merge_kernel(ya_ref, zb_ref, yc_ref, ga_ref, gb_ref, gc_ref, wa_ref, wb_ref, wc_ref, o_ref):
    pa = jnp.dot(ya_ref[...], wa_ref[...], preferred_element_type=F32)
    pb = jnp.dot(zb_ref[...], wb_ref[...], preferred_element_type=F32)
    pc = jnp.dot(yc_ref[...], wc_ref[...], preferred_element_type=F32)
    m = ga_ref[...].astype(F32) * pa + gb_ref[...].astype(F32) * pb + gc_ref[...].astype(F32) * pc
    o_ref[...] = m.astype(o_ref.dtype)


def gated_merge(ya, zb, yc, gates, w_pa, w_pb, w_pc):
    T = ya.shape[0]
    D = w_pa.shape[1]
    tm, tn = _tile(TILE_M, T), _tile(TILE_N, D)
    nb = D // tn
    row = lambda a: pl.BlockSpec((tm, a.shape[1]), lambda i, j: (i, 0))
    col = lambda w: pl.BlockSpec((w.shape[0], tn), lambda i, j: (0, j))
    gate = lambda b: pl.BlockSpec((tm, tn), lambda i, j: (i, b * nb + j))
    return pl.pallas_call(
        _merge_kernel,
        out_shape=jax.ShapeDtypeStruct((T, D), BF16),
        grid=(T // tm, nb),
        in_specs=[row(ya), row(zb), row(yc), gate(0), gate(1), gate(2), col(w_pa), col(w_pb), col(w_pc)],
        out_specs=pl.BlockSpec((tm, tn), lambda i, j: (i, j)),
        compiler_params=_params("parallel", "arbitrary"),
        name="gated_merge",
    )(ya, zb, yc, gates, gates, gates, w_pa, w_pb, w_pc)


def _pe_kernel(x_ref, g_ref, wg_ref, pe_ref, wp_ref, r_ref, o_ref, xn_ref):
    @pl.when(pl.program_id(1) == 0)
    def _():
        xn_ref[...] = _rms_rows(x_ref[...], g_ref[...]).astype(BF16)

    gate = jax.nn.sigmoid(jnp.dot(xn_ref[...], wg_ref[...], preferred_element_type=F32))
    emb = jnp.dot(pe_ref[...], wp_ref[...], preferred_element_type=F32)
    o_ref[...] = r_ref[...] + gate * emb


def pe_inject(h, g, w_gate, pe, w_proj):
    T, D = h.shape
    P = pe.shape[1]
    tm, tn = _tile(TILE_M, T), _tile(TILE_N, D)
    return pl.pallas_call(
        _pe_kernel,
        out_shape=jax.ShapeDtypeStruct((T, D), F32),
        grid=(T // tm, D // tn),
        in_specs=[
            pl.BlockSpec((tm, D), lambda i, j: (i, 0)),
            pl.BlockSpec((1, D), lambda i, j: (0, 0)),
            pl.BlockSpec((D, tn), lambda i, j: (0, j)),
            pl.BlockSpec((tm, P), lambda i, j: (i, 0)),
            pl.BlockSpec((P, tn), lambda i, j: (0, j)),
            pl.BlockSpec((tm, tn), lambda i, j: (i, j)),
        ],
        out_specs=pl.BlockSpec((tm, tn), lambda i, j: (i, j)),
        scratch_shapes=[pltpu.VMEM((tm, D), BF16)],
        compiler_params=_params("parallel", "arbitrary"),
        name="pe_inject",
    )(h, g.reshape(1, D), w_gate, pe, w_proj, h)


def _rmsnorm_kernel(x_ref, g_ref, o_ref):
    o_ref[...] = _rms_rows(x_ref[...], g_ref[...])


def rms_norm(x, g):
    T, D = x.shape
    tm = _tile(TILE_M_DOWN, T)
    return pl.pallas_call(
        _rmsnorm_kernel,
        out_shape=jax.ShapeDtypeStruct((T, D), F32),
        grid=(T // tm,),
        in_specs=[pl.BlockSpec((tm, D), lambda i: (i, 0)), pl.BlockSpec((1, D), lambda i: (0, 0))],
        out_specs=pl.BlockSpec((tm, D), lambda i: (i, 0)),
        compiler_params=_params("parallel"),
        name="rms_norm",
    )(x, g.reshape(1, D))


def _conv3_rows(x, prev_row, next_row, w):
    tl = x.shape[0]
    rows = lax.broadcasted_iota(jnp.int32, x.shape, 0)
    xm = jnp.where(rows == 0, prev_row, pltpu.roll(x, 1, 0))
    xp = jnp.where(rows == tl - 1, next_row, pltpu.roll(x, tl - 1, 0))
    return xm * w[0:1] + x * w[1:2] + xp * w[2:3]


def _halo_specs(tl, tc, n_l, col_block):
    hb = tl // BF16_SUBLANES
    n_h = n_l * hb
    main = pl.BlockSpec((1, tl, tc), lambda b, l, c: (b, l, col_block(c)))
    prev = pl.BlockSpec((1, BF16_SUBLANES, tc), lambda b, l, c: (b, jnp.maximum(l * hb - 1, 0), col_block(c)))
    nxt = pl.BlockSpec((1, BF16_SUBLANES, tc), lambda b, l, c: (b, jnp.minimum((l + 1) * hb, n_h - 1), col_block(c)))
    return main, prev, nxt


def _edge_rows(prev_ref, next_ref):
    l = pl.program_id(1)
    prev = prev_ref[0].astype(F32)[BF16_SUBLANES - 1:BF16_SUBLANES]
    nxt = next_ref[0].astype(F32)[0:1]
    prev = jnp.where(l == 0, 0.0, prev)
    nxt = jnp.where(l == pl.num_programs(1) - 1, 0.0, nxt)
    return prev, nxt


def _conv_a_kernel(b_ref, c_ref, cp_ref, cn_ref, x_ref, xp_ref, xn_ref, w_ref, o_ref):
    cprev, cnext = _edge_rows(cp_ref, cn_ref)
    xprev, xnext = _edge_rows(xp_ref, xn_ref)
    p = c_ref[0].astype(F32) * x_ref[0].astype(F32)
    y = _conv3_rows(p, cprev * xprev, cnext * xnext, w_ref[...])
    o_ref[0] = (b_ref[0].astype(F32) * y).astype(o_ref.dtype)


def conv_a(proj, w, dc):
    B, L, _ = proj.shape
    tl, tc = _tile(TILE_L, L), _tile(TILE_C, dc)
    nc, nl = dc // tc, L // tl
    b_spec = pl.BlockSpec((1, tl, tc), lambda b, l, c: (b, l, c))
    c_main, c_prev, c_next = _halo_specs(tl, tc, nl, lambda c: nc + c)
    x_main, x_prev, x_next = _halo_specs(tl, tc, nl, lambda c: 2 * nc + c)
    return pl.pallas_call(
        _conv_a_kernel,
        out_shape=jax.ShapeDtypeStruct((B, L, dc), BF16),
        grid=(B, nl, nc),
        in_specs=[b_spec, c_main, c_prev, c_next, x_main, x_prev, x_next,
                  pl.BlockSpec((3, tc), lambda b, l, c: (0, c))],
        out_specs=pl.BlockSpec((1, tl, tc), lambda b, l, c: (b, l, c)),
        compiler_params=_params("parallel", "arbitrary", "arbitrary"),
        name="conv_a",
    )(proj, proj, proj, proj, proj, proj, proj, w)


def _hy_short_kernel(*refs):
    ins, w_refs, outs = refs[:9], refs[9:12], refs[12:]
    for s in range(3):
        x_ref, p_ref, n_ref = ins[3 * s:3 * s + 3]
        prev, nxt = _edge_rows(p_ref, n_ref)
        y = _conv3_rows(x_ref[0].astype(F32), prev, nxt, w_refs[s][...])
        outs[s][0] = y.astype(outs[s].dtype)


def hyena_short_conv(proj, w, col0, dh):
    B, L, _ = proj.shape
    tl, tc = _tile(TILE_L, L), _tile(TILE_C, dh)
    nc, nl = dh // tc, L // tl
    base = col0 // tc
    in_specs, w_specs = [], []
    for s in range(3):
        in_specs += list(_halo_specs(tl, tc, nl, lambda c, s=s: base + s * nc + c))
        w_specs.append(pl.BlockSpec((3, tc), lambda b, l, c, s=s: (0, s * nc + c)))
    out_spec = pl.BlockSpec((1, tl, tc), lambda b, l, c: (b, l, c))
    out = jax.ShapeDtypeStruct((B, L, dh), BF16)
    return pl.pallas_call(
        _hy_short_kernel,
        out_shape=(out, out, out),
        grid=(B, nl, nc),
        in_specs=in_specs + w_specs,
        out_specs=(out_spec, out_spec, out_spec),
        compiler_params=_params("parallel", "arbitrary", "arbitrary"),
        name="hyena_short_conv",
    )(*([proj] * 9), w, w, w)


def _norm_rope(x, g, cos_t, sin_t):
    xn = _rms_rows(x, g)
    lane = lax.broadcasted_iota(jnp.int32, xn.shape, 1)
    quarter = xn.shape[1] // 4
    swapped = jnp.where((lane % (2 * quarter)) < quarter,
                        pltpu.roll(xn, 3 * quarter, 1), pltpu.roll(xn, quarter, 1))
    return xn * cos_t + swapped * sin_t


def _q_prep_kernel(x_ref, g_ref, cos_ref, sin_ref, o_ref, *, scale):
    y = _norm_rope(x_ref[0].astype(F32), g_ref[...], cos_ref[...], sin_ref[...])
    o_ref[0, 0] = (y * scale).astype(o_ref.dtype)


def q_prep(proj, gain, cos_t, sin_t, col0, n_q, hd):
    B, L, _ = proj.shape
    tl = _tile(TILE_L, L)
    base = col0 // hd
    return pl.pallas_call(
        functools.partial(_q_prep_kernel, scale=hd ** -0.5),
        out_shape=jax.ShapeDtypeStruct((B, n_q, L, hd), BF16),
        grid=(B, L // tl, n_q),
        in_specs=[
            pl.BlockSpec((1, tl, hd), lambda b, l, h: (b, l, base + h)),
            pl.BlockSpec((1, hd), lambda b, l, h: (0, 0)),
            pl.BlockSpec((tl, hd), lambda b, l, h: (l, 0)),
            pl.BlockSpec((tl, hd), lambda b, l, h: (l, 0)),
        ],
        out_specs=pl.BlockSpec((1, 1, tl, hd), lambda b, l, h: (b, h, l, 0)),
        compiler_params=_params("parallel", "arbitrary", "arbitrary"),
        name="q_prep",
    )(proj, gain.reshape(1, hd), cos_t, sin_t)


def _kv_prep_kernel(k_ref, v_ref, g_ref, cos_ref, sin_ref, kt_ref, vx_ref):
    y = _norm_rope(k_ref[0].astype(F32), g_ref[...], cos_ref[...], sin_ref[...])
    kt_ref[0, 0, 0] = y.T.astype(kt_ref.dtype)
    v = v_ref[0]
    vx_ref[0, 0] = jnp.concatenate([v, jnp.ones_like(v)], axis=1)


def kv_prep(proj, gain, cos_t, sin_t, k_col0, v_col0, n_kv, hd, tk):
    B, L, _ = proj.shape
    k_base, v_base = k_col0 // hd, v_col0 // hd
    return pl.pallas_call(
        _kv_prep_kernel,
        out_shape=(jax.ShapeDtypeStruct((B, n_kv, L // tk, hd, tk), BF16),
                   jax.ShapeDtypeStruct((B, n_kv, L, 2 * hd), BF16)),
        grid=(B, L // tk, n_kv),
        in_specs=[
            pl.BlockSpec((1, tk, hd), lambda b, l, h: (b, l, k_base + h)),
            pl.BlockSpec((1, tk, hd), lambda b, l, h: (b, l, v_base + h)),
            pl.BlockSpec((1, hd), lambda b, l, h: (0, 0)),
            pl.BlockSpec((tk, hd), lambda b, l, h: (l, 0)),
            pl.BlockSpec((tk, hd), lambda b, l, h: (l, 0)),
        ],
        out_specs=(pl.BlockSpec((1, 1, 1, hd, tk), lambda b, l, h: (b, h, l, 0, 0)),
                   pl.BlockSpec((1, 1, tk, 2 * hd), lambda b, l, h: (b, h, l, 0))),
        compiler_params=_params("parallel", "arbitrary", "arbitrary"),
        name="kv_prep",
    )(proj, proj, gain.reshape(1, hd), cos_t, sin_t)


def _flash_kernel(q_ref, kt_ref, vx_ref, o_ref, s_ref, m_ref, acc_ref):
    group, tq, hd = q_ref.shape[1:]
    nk, _, tk = kt_ref.shape[2:]
    q = q_ref[0].reshape(group * tq, hd)

    def scores(j):
        return jnp.dot(q, kt_ref[0, 0, j], preferred_element_type=F32)

    m_ref[...] = jnp.full(m_ref.shape, -jnp.inf, F32)
    acc_ref[...] = jnp.zeros(acc_ref.shape, F32)
    s_ref[0] = scores(0)

    def step(j, slot):
        s_ref[1 - slot] = scores(jnp.minimum(j + 1, nk - 1))
        s = s_ref[slot]
        m_prev = m_ref[...]
        m_new = jnp.maximum(m_prev, jnp.broadcast_to(jnp.max(s, axis=1, keepdims=True), m_prev.shape))
        alpha = jnp.exp(m_prev - m_new)
        p = jnp.exp(s - jnp.concatenate([m_new] * (tk // LANES), axis=1))
        vj = vx_ref[0, 0, pl.ds(pl.multiple_of(j * tk, tk), tk), :]
        acc_ref[...] = (jnp.concatenate([alpha] * (2 * hd // LANES), axis=1) * acc_ref[...]
                        + jnp.dot(p.astype(BF16), vj, preferred_element_type=F32))
        m_ref[...] = m_new

    def body(i, carry):
        step(2 * i, 0)
        step(2 * i + 1, 1)
        return carry

    lax.fori_loop(0, nk // 2, body, 0)
    acc = acc_ref[...]
    out = acc[:, :hd] / acc[:, hd:]
    for g in range(group):
        o_ref[0, :, g * hd:(g + 1) * hd] = out[g * tq:(g + 1) * tq].astype(o_ref.dtype)


def flash_attention(q, kt, vx):
    B, n_q, L, hd = q.shape
    _, n_kv, nk, _, tk = kt.shape
    assert hd == LANES and nk % 2 == 0
    group = n_q // n_kv
    tq = _tile(TILE_Q, L)
    return pl.pallas_call(
        _flash_kernel,
        out_shape=jax.ShapeDtypeStruct((B, L, n_q * hd), BF16),
        grid=(B, n_kv, L // tq),
        in_specs=[
            pl.BlockSpec((1, group, tq, hd), lambda b, h, i: (b, h, i, 0)),
            pl.BlockSpec((1, 1, nk, hd, tk), lambda b, h, i: (b, h, 0, 0, 0)),
            pl.BlockSpec((1, 1, L, 2 * hd), lambda b, h, i: (b, h, 0, 0)),
        ],
        out_specs=pl.BlockSpec((1, tq, group * hd), lambda b, h, i: (b, i, h)),
        scratch_shapes=[
            pltpu.VMEM((2, group * tq, tk), F32),
            pltpu.VMEM((group * tq, LANES), F32),
            pltpu.VMEM((group * tq, 2 * hd), F32),
        ],
        compiler_params=_params("parallel", "parallel", "arbitrary"),
        name="flash_attention",
    )(q, kt, vx)


def _hyena_filter_kernel(z_ref, w1_ref, b1_ref, f1_ref, w2_ref, b2_ref, f2_ref, w3_ref, dec_ref, bwd_ref,
                         h_ref, l1_ref):
    i = pl.program_id(0)
    hi = lax.Precision.HIGHEST
    z = z_ref[...]
    h = jnp.sin(f1_ref[...] * (jnp.dot(z, w1_ref[...], precision=hi, preferred_element_type=F32) + b1_ref[...]))
    h = jnp.sin(f2_ref[...] * (jnp.dot(h, w2_ref[...], precision=hi, preferred_element_type=F32) + b2_ref[...]))
    h = jnp.dot(h, w3_ref[...], precision=hi, preferred_element_type=F32)
    h = h * jnp.exp(-z[:, 0:1] * jnp.abs(dec_ref[...]))
    rows = lax.broadcasted_iota(jnp.int32, h.shape, 0)
    h = jnp.where((rows == 0) & (i == 0) & (bwd_ref[...] > 0.5), 0.0, h)

    @pl.when(i == 0)
    def _():
        l1_ref[...] = jnp.zeros(l1_ref.shape, F32)

    l1_ref[...] += jnp.sum(jnp.abs(h), axis=0, keepdims=True)
    h_ref[...] = h.astype(h_ref.dtype)


def hyena_filter_taps(zfeat, w1, b1, f1, w2, b2, f2, w3, decay, bwd_mask):
    L, E = zfeat.shape
    H = w2.shape[0]
    C4 = w3.shape[1]
    tt = _tile(TILE_HF, L)
    full = lambda a: pl.BlockSpec(a.shape, lambda i: (0, 0))
    args = (zfeat, w1, b1.reshape(1, H), f1.reshape(1, H), w2, b2.reshape(1, H), f2.reshape(1, H), w3,
            decay.reshape(1, C4), bwd_mask.reshape(1, C4))
    return pl.pallas_call(
        _hyena_filter_kernel,
        out_shape=(jax.ShapeDtypeStruct((L, C4), BF16), jax.ShapeDtypeStruct((1, C4), F32)),
        grid=(L // tt,),
        in_specs=[pl.BlockSpec((tt, E), lambda i: (i, 0))] + [full(a) for a in args[1:]],
        out_specs=(pl.BlockSpec((tt, C4), lambda i: (i, 0)), pl.BlockSpec((1, C4), lambda i: (0, 0))),
        compiler_params=_params("arbitrary"),
        name="hyena_filter_taps",
    )(*args)


def _dft_stage1_kernel(w_ref, z_ref, o_ref):
    o_ref[0] = jnp.dot(w_ref[...], z_ref[0], preferred_element_type=F32).astype(o_ref.dtype)


def dft_stage1(w1, z2):
    B, K, cols = z2.shape
    M = w1.shape[0]
    tn = _tile(TILE_FFT_COLS, cols)
    return pl.pallas_call(
        _dft_stage1_kernel,
        out_shape=jax.ShapeDtypeStruct((B, M, cols), BF16),
        grid=(B, cols // tn),
        in_specs=[pl.BlockSpec((M, K), lambda b, j: (0, 0)), pl.BlockSpec((1, K, tn), lambda b, j: (b, 0, j))],
        out_specs=pl.BlockSpec((1, M, tn), lambda b, j: (b, 0, j)),
        compiler_params=_params("parallel", "arbitrary"),
        name="dft_stage1",
    )(w1, z2)


def _filter_spectrum_kernel(g_ref, af_ref, ab_ref, lf_ref, lb_ref, o_ref):
    n1 = af_ref.shape[3]
    inv_l1 = 1.0 / (lf_ref[...] + lb_ref[...])
    for kk in range(af_ref.shape[1]):
        g = g_ref[kk]
        xf = jnp.dot(g, af_ref[0, kk].reshape(2 * n1, -1), preferred_element_type=F32)
        xb = jnp.dot(g, ab_ref[0, kk].reshape(2 * n1, -1), preferred_element_type=F32)
        o_ref[0, kk, 0] = (xf[:n1] + xb[:n1]) * inv_l1
        o_ref[0, kk, 1] = (xf[n1:] - xb[n1:]) * inv_l1


def filter_spectrum(g_tab, a5, l1, n_order, dh):
    _, k2p, _, n1, _ = a5.shape
    kb, cb = _tile(TILE_K2, k2p), _tile(TILE_FFT_C, dh)
    nc = dh // cb
    a_spec = lambda d: pl.BlockSpec((1, kb, 2, n1, cb), lambda k, o, c: (0, k, 0, 0, (2 * o + d) * nc + c))
    l_spec = lambda d: pl.BlockSpec((1, cb), lambda k, o, c: (0, (2 * o + d) * nc + c))
    return pl.pallas_call(
        _filter_spectrum_kernel,
        out_shape=jax.ShapeDtypeStruct((n_order, k2p, 2, n1, dh), F32),
        grid=(k2p // kb, n_order, nc),
        in_specs=[pl.BlockSpec((kb, 2 * n1, 2 * n1), lambda k, o, c: (k, 0, 0)),
                  a_spec(0), a_spec(1), l_spec(0), l_spec(1)],
        out_specs=pl.BlockSpec((1, kb, 2, n1, cb), lambda k, o, c: (o, k, 0, 0, c)),
        compiler_params=_params("parallel", "arbitrary", "arbitrary"),
        name="filter_spectrum",
    )(g_tab, a5, a5, l1, l1)


def _dft_stage2_kernel(g_ref, gt_ref, a_ref, kf_ref, o_ref):
    n1 = a_ref.shape[3]
    for kk in range(a_ref.shape[1]):
        x = jnp.dot(g_ref[kk], a_ref[0, kk].reshape(2 * n1, -1), preferred_element_type=F32)
        xr, xi = x[:n1], x[n1:]
        kr, ki = kf_ref[0, kk, 0], kf_ref[0, kk, 1]
        p = jnp.concatenate([xr * kr - xi * ki, xr * ki + xi * kr], axis=0).astype(BF16)
        q = jnp.dot(gt_ref[kk], p, preferred_element_type=F32)
        o_ref[0, kk] = q.reshape(2, n1, -1).astype(o_ref.dtype)


def dft_stage2(g_tab, gt_tab, a5, kf, order):
    B, k2p, _, n1, C = a5.shape
    kb, cb = _tile(TILE_K2, k2p), _tile(TILE_FFT_C, C)
    g_spec = pl.BlockSpec((kb, 2 * n1, 2 * n1), lambda k, c, b: (k, 0, 0))
    return pl.pallas_call(
        _dft_stage2_kernel,
        out_shape=jax.ShapeDtypeStruct(a5.shape, BF16),
        grid=(k2p // kb, C // cb, B),
        in_specs=[g_spec, g_spec,
                  pl.BlockSpec((1, kb, 2, n1, cb), lambda k, c, b: (b, k, 0, 0, c)),
                  pl.BlockSpec((1, kb, 2, n1, cb), lambda k, c, b: (order, k, 0, 0, c))],
        out_specs=pl.BlockSpec((1, kb, 2, n1, cb), lambda k, c, b: (b, k, 0, 0, c)),
        compiler_params=_params("parallel", "arbitrary", "arbitrary"),
        name="dft_stage2",
    )(g_tab, gt_tab, a5, kf)


def _dft_stage3_kernel(w_ref, q_ref, z_ref, gate_ref, skip_ref, o_ref):
    y = jnp.dot(w_ref[...], q_ref[0], preferred_element_type=F32)
    z = z_ref[0].astype(F32)
    o_ref[0] = (gate_ref[0].astype(F32) * (y + z * skip_ref[...])).astype(o_ref.dtype)


def dft_stage3(wi, q2, z2, gate2, skip_cols):
    B, M, cols = q2.shape
    n2h = wi.shape[0]
    tn = _tile(TILE_FFT_COLS, cols)
    blk = pl.BlockSpec((1, n2h, tn), lambda b, j: (b, 0, j))
    return pl.pallas_call(
        _dft_stage3_kernel,
        out_shape=jax.ShapeDtypeStruct((B, n2h, cols), BF16),
        grid=(B, cols // tn),
        in_specs=[pl.BlockSpec((n2h, M), lambda b, j: (0, 0)), pl.BlockSpec((1, M, tn), lambda b, j: (b, 0, j)),
                  blk, blk, pl.BlockSpec((1, tn), lambda b, j: (0, j))],
        out_specs=blk,
        compiler_params=_params("parallel", "arbitrary"),
        name="dft_stage3",
    )(wi, q2, z2, gate2, skip_cols)


def _dft_tables(L):
    n1 = FFT_N1
    n2h = L // n1
    n2 = 2 * n2h
    n = 2 * L
    k2 = n2h + 1
    k2p = -(-k2 // 8) * 8
    kk = np.arange(k2)
    ang1 = 2.0 * np.pi * np.outer(kk, np.arange(n2h)) / n2
    w1 = np.zeros((k2p, 2, n2h))
    w1[:k2, 0], w1[:k2, 1] = np.cos(ang1), -np.sin(ang1)
    c = np.where((kk == 0) | (kk == n2h), 1.0, 2.0)
    wi = np.zeros((n2h, k2p, 2))
    wi[:, :k2, 0], wi[:, :k2, 1] = (c[:, None] * np.cos(ang1)).T / n, (-c[:, None] * np.sin(ang1)).T / n
    freq = kk[:, None, None] + n2 * np.arange(n1)[None, :, None]
    ang2 = 2.0 * np.pi * freq * np.arange(n1)[None, None, :] / n
    gr, gi = np.cos(ang2), -np.sin(ang2)
    g = np.zeros((k2p, 2 * n1, 2 * n1))
    g[:k2] = np.concatenate([np.concatenate([gr, -gi], axis=2), np.concatenate([gi, gr], axis=2)], axis=1)
    to = lambda a: jnp.asarray(a.astype(np.float32)).astype(BF16)
    return to(w1.reshape(2 * k2p, n2h)), to(wi.reshape(n2h, 2 * k2p)), to(g), to(g.transpose(0, 2, 1)), k2p


def _rope_tables(L, hd):
    quarter = hd // 4
    rows_n = L // GRID_W
    row = jnp.repeat(jnp.arange(rows_n, dtype=F32), GRID_W)
    col = jnp.tile(jnp.arange(GRID_W, dtype=F32), rows_n)
    half = hd // 2
    inv = ROPE_THETA ** (-jnp.arange(0, half, 2, dtype=F32) / half)
    ar, ac = row[:, None] * inv[None, :], col[:, None] * inv[None, :]
    cos_t = jnp.concatenate([jnp.cos(ar), jnp.cos(ar), jnp.cos(ac), jnp.cos(ac)], axis=-1)
    sin_t = jnp.concatenate([-jnp.sin(ar), jnp.sin(ar), -jnp.sin(ac), jnp.sin(ac)], axis=-1)
    assert cos_t.shape == (L, 4 * quarter)
    return cos_t, sin_t


def _hyena_pos_features(L, emb):
    bands = (emb - 1) // 2
    t = jnp.linspace(0.0, 1.0, L, dtype=F32)[:, None]
    w = 2.0 * math.pi * jnp.arange(L, dtype=F32)[:, None] / L
    f = jnp.linspace(1e-4, bands - 1, bands, dtype=F32)[None, :]
    ang = w * f
    return jnp.concatenate([t, jnp.cos(ang), -jnp.sin(ang)], axis=-1)


def kernel(x_prompt, x_sample, p_prompt, p_sample, n_ffn1, w_ffn1_up, w_ffn1_down, n_mix, w_in, w_gate, conv_a_w, hy_short_w, hf_w1, hf_b1, hf_freq1, hf_w2, hf_b2, hf_freq2, hf_w3, hf_decay, hf_skip, q_norm, k_norm, w_pa, w_pb, w_pc, w_o, n_ffn2, w_ffn2_up, w_ffn2_down, n_pe, w_pe_gate, w_pe_proj, final_norm):
    assert x_prompt.shape[1:] == x_sample.shape[1:]
    bp, L, D = x_prompt.shape
    B = bp + x_sample.shape[0]
    T = B * L
    depth = w_in.shape[0]
    dc = conv_a_w.shape[-1]
    n_order, dh = hf_skip.shape[1:]
    hd = q_norm.shape[-1]
    qw = w_pc.shape[1]
    kvw = (w_in.shape[-1] - 3 * dc - 3 * dh - qw) // 2
    n_q, n_kv = qw // hd, kvw // hd
    emb = hf_w1.shape[1]
    assert n_order == 2 and hf_w3.shape[-1] == 4 * dh and L % (FFT_N1 * 8) == 0 and L % GRID_W == 0

    h = jnp.concatenate([x_prompt, x_sample], axis=0).reshape(T, D)
    pe_all = jnp.concatenate([p_prompt, p_sample], axis=1).astype(BF16).reshape(depth, T, -1)

    w1_tab, wi_tab, g_tab, gt_tab, k2p = _dft_tables(L)
    cos_t, sin_t = _rope_tables(L, hd)
    emb_p = -(-emb // 8) * 8
    zfeat = jnp.pad(_hyena_pos_features(L, emb), ((0, 0), (0, emb_p - emb)))
    bwd_mask = jnp.tile(jnp.concatenate([jnp.zeros((dh,), F32), jnp.ones((dh,), F32)]), n_order)
    n2h = L // FFT_N1
    fft_cols = FFT_N1 * dh

    def long_conv(z, gate, kf, order, skip):
        a = dft_stage1(w1_tab, z.reshape(B, n2h, fft_cols))
        q = dft_stage2(g_tab, gt_tab, a.reshape(B, k2p, 2, FFT_N1, dh), kf, order)
        y = dft_stage3(wi_tab, q.reshape(B, 2 * k2p, fft_cols), z.reshape(B, n2h, fft_cols),
                       gate.reshape(B, n2h, fft_cols), jnp.tile(skip, FFT_N1).reshape(1, fft_cols))
        return y.reshape(B, L, dh)

    for i in range(depth):
        bf = lambda w: w[i].astype(BF16)
        act = norm_swiglu_up(h, n_ffn1[i], bf(w_ffn1_up))
        h = matmul_residual(act, bf(w_ffn1_down), h, 0.5)
        proj = norm_matmul(h, n_mix[i], bf(w_in)).reshape(B, L, -1)
        gates = norm_matmul(h, n_mix[i], bf(w_gate), act="sigmoid")
        ya = conv_a(proj, conv_a_w[i], dc)
        hv, hx1, hx2 = hyena_short_conv(proj, hy_short_w[i], 3 * dc, dh)
        w1p = jnp.pad(hf_w1[i], ((0, emb_p - emb), (0, 0)))
        taps, l1 = hyena_filter_taps(zfeat, w1p, hf_b1[i], hf_freq1[i], hf_w2[i], hf_b2[i], hf_freq2[i],
                                     hf_w3[i], hf_decay[i], bwd_mask)
        af = dft_stage1(w1_tab, taps.reshape(1, n2h, FFT_N1 * 4 * dh))
        kf = filter_spectrum(g_tab, af.reshape(1, k2p, 2, FFT_N1, 4 * dh), l1, n_order, dh)
        zb = long_conv(hv, hx1, kf, 0, hf_skip[i, 0])
        zb = long_conv(zb, hx2, kf, 1, hf_skip[i, 1])
        q_col0 = 3 * dc + 3 * dh
        qr = q_prep(proj, q_norm[i], cos_t, sin_t, q_col0, n_q, hd)
        kt, vx = kv_prep(proj, k_norm[i], cos_t, sin_t, q_col0 + qw, q_col0 + qw + kvw, n_kv, hd,
                         _tile(TILE_K, L))
        yc = flash_attention(qr, kt, vx)
        m = gated_merge(ya.reshape(T, dc), zb.reshape(T, dh), yc.reshape(T, qw), gates,
                        bf(w_pa), bf(w_pb), bf(w_pc))
        h = matmul_residual(m, bf(w_o), h, 1.0)
        act = norm_swiglu_up(h, n_ffn2[i], bf(w_ffn2_up))
        h = matmul_residual(act, bf(w_ffn2_down), h, 0.5)
        h = pe_inject(h, n_pe[i], bf(w_pe_gate), pe_all[i], bf(w_pe_proj))

    y = rms_norm(h, final_norm).reshape(B, L, D)
    return (y[:bp], y[bp:])
```

```python
import functools
import math

import numpy as np
import jax
import jax.numpy as jnp
from jax import lax
from jax.experimental import pallas as pl
from jax.experimental.pallas import tpu as pltpu

F32 = jnp.float32
BF16 = jnp.bfloat16

RMS_EPS = 1e-6
ROPE_THETA = 10000.0
GRID_W = 64
LANES = 128
BF16_SUBLANES = 16
FFT_N1 = 128
VMEM_LIMIT_BYTES = 52 * 1024 * 1024

TILE_M = 1024
TILE_M_DOWN = 1024
TILE_N = 512
TILE_L = 1024
TILE_C = 512
TILE_Q = 256
TILE_K = 1024
TILE_K2 = 8
TILE_FFT_C = 512
TILE_HF = 256


def _tile(default, dim):
    t = min(default, dim)
    assert dim % t == 0, (default, dim)
    return t


def _params(*sem):
    return pltpu.CompilerParams(dimension_semantics=sem, vmem_limit_bytes=VMEM_LIMIT_BYTES)


def _rms_rows(x, g):
    ms = jnp.mean(x * x, axis=-1, keepdims=True)
    return x * lax.rsqrt(ms + RMS_EPS) * g


def _norm_mm_kernel(x_ref, g_ref, w_ref, o_ref, xn_ref, *, act):
    @pl.when(pl.program_id(1) == 0)
    def _():
        xn_ref[...] = _rms_rows(x_ref[...], g_ref[...]).astype(BF16)

    y = jnp.dot(xn_ref[...], w_ref[...], preferred_element_type=F32)
    if act == "sigmoid":
        y = jax.nn.sigmoid(y)
    o_ref[...] = y.astype(o_ref.dtype)


def norm_matmul(x, g, w, act=None):
    T, D = x.shape
    N = w.shape[1]
    tm, tn = _tile(TILE_M, T), _tile(TILE_N, N)
    return pl.pallas_call(
        functools.partial(_norm_mm_kernel, act=act),
        out_shape=jax.ShapeDtypeStruct((T, N), BF16),
        grid=(T // tm, N // tn),
        in_specs=[
            pl.BlockSpec((tm, D), lambda i, j: (i, 0)),
            pl.BlockSpec((1, D), lambda i, j: (0, 0)),
            pl.BlockSpec((D, tn), lambda i, j: (0, j)),
        ],
        out_specs=pl.BlockSpec((tm, tn), lambda i, j: (i, j)),
        scratch_shapes=[pltpu.VMEM((tm, D), BF16)],
        compiler_params=_params("parallel", "arbitrary"),
        name="norm_matmul",
    )(x, g.reshape(1, D), w)


def _norm_swiglu_kernel(x_ref, g_ref, wa_ref, wb_ref, o_ref, xn_ref):
    @pl.when(pl.program_id(1) == 0)
    def _():
        xn_ref[...] = _rms_rows(x_ref[...], g_ref[...]).astype(BF16)

    xn = xn_ref[...]
    a = jnp.dot(xn, wa_ref[...], preferred_element_type=F32)
    b = jnp.dot(xn, wb_ref[...], preferred_element_type=F32)
    o_ref[...] = (a * jax.nn.sigmoid(a) * b).astype(o_ref.dtype)


def norm_swiglu_up(x, g, w_up):
    T, D = x.shape
    F = w_up.shape[1] // 2
    tm, tn = _tile(TILE_M, T), _tile(TILE_N, F)
    nb = F // tn
    return pl.pallas_call(
        _norm_swiglu_kernel,
        out_shape=jax.ShapeDtypeStruct((T, F), BF16),
        grid=(T // tm, nb),
        in_specs=[
            pl.BlockSpec((tm, D), lambda i, j: (i, 0)),
            pl.BlockSpec((1, D), lambda i, j: (0, 0)),
            pl.BlockSpec((D, tn), lambda i, j: (0, j)),
            pl.BlockSpec((D, tn), lambda i, j: (0, nb + j)),
        ],
        out_specs=pl.BlockSpec((tm, tn), lambda i, j: (i, j)),
        scratch_shapes=[pltpu.VMEM((tm, D), BF16)],
        compiler_params=_params("parallel", "arbitrary"),
        name="norm_swiglu_up",
    )(x, g.reshape(1, D), w_up, w_up)


def _mm_residual_kernel(a_ref, w_ref, r_ref, o_ref, *, scale):
    y = jnp.dot(a_ref[...], w_ref[...], preferred_element_type=F32)
    o_ref[...] = r_ref[...] + scale * y


def matmul_residual(a, w, res, scale):
    T, K = a.shape
    N = w.shape[1]
    tm, tn = _tile(TILE_M_DOWN, T), _tile(TILE_N, N)
    return pl.pallas_call(
        functools.partial(_mm_residual_kernel, scale=scale),
        out_shape=jax.ShapeDtypeStruct((T, N), F32),
        grid=(T // tm, N // tn),
        in_specs=[
            pl.BlockSpec((tm, K), lambda i, j: (i, 0)),
            pl.BlockSpec((K, tn), lambda i, j: (0, j)),
            pl.BlockSpec((tm, tn), lambda i, j: (i, j)),
        ],
        out_specs=pl.BlockSpec((tm, tn), lambda i, j: (i, j)),
        compiler_params=_params("parallel", "arbitrary"),
        name="matmul_residual",
    )(a, w, res)


def _merge_kernel(ya_ref, zb_ref, yc_ref, ga_ref, gb_ref, gc_ref, wa_ref, wb_ref, wc_ref, o_ref):
    pa = jnp.dot(ya_ref[...], wa_ref[...], preferred_element_type=F32)
    pb = jnp.dot(zb_ref[...], wb_ref[...], preferred_element_type=F32)
    pc = jnp.dot(yc_ref[...], wc_ref[...], preferred_element_type=F32)
    m = ga_ref[...].astype(F32) * pa + gb_ref[...].astype(F32) * pb + gc_ref[...].astype(F32) * pc
    o_ref[...] = m.astype(o_ref.dtype)


def gated_merge(ya, zb, yc, gates, w_pa, w_pb, w_pc):
    T = ya.shape[0]
    D = w_pa.shape[1]
    tm, tn = _tile(TILE_M, T), _tile(TILE_N, D)
    nb = D // tn
    row = lambda a: pl.BlockSpec((tm, a.shape[1]), lambda i, j: (i, 0))
    col = lambda w: pl.BlockSpec((w.shape[0], tn), lambda i, j: (0, j))
    gate = lambda b: pl.BlockSpec((tm, tn), lambda i, j: (i, b * nb + j))
    return pl.pallas_call(
        _merge_kernel,
        out_shape=jax.ShapeDtypeStruct((T, D), BF16),
        grid=(T // tm, nb),
        in_specs=[row(ya), row(zb), row(yc), gate(0), gate(1), gate(2), col(w_pa), col(w_pb), col(w_pc)],
        out_specs=pl.BlockSpec((tm, tn), lambda i, j: (i, j)),
        compiler_params=_params("parallel", "arbitrary"),
        name="gated_merge",
    )(ya, zb, yc, gates, gates, gates, w_pa, w_pb, w_pc)


def _pe_kernel(x_ref, g_ref, wg_ref, pe_ref, wp_ref, r_ref, o_ref, xn_ref):
    @pl.when(pl.program_id(1) == 0)
    def _():
        xn_ref[...] = _rms_rows(x_ref[...], g_ref[...]).astype(BF16)

    gate = jax.nn.sigmoid(jnp.dot(xn_ref[...], wg_ref[...], preferred_element_type=F32))
    emb = jnp.dot(pe_ref[...], wp_ref[...], preferred_element_type=F32)
    o_ref[...] = r_ref[...] + gate * emb


def pe_inject(h, g, w_gate, pe, w_proj):
    T, D = h.shape
    P = pe.shape[1]
    tm, tn = _tile(TILE_M, T), _tile(TILE_N, D)
    return pl.pallas_call(
        _pe_kernel,
        out_shape=jax.ShapeDtypeStruct((T, D), F32),
        grid=(T // tm, D // tn),
        in_specs=[
            pl.BlockSpec((tm, D), lambda i, j: (i, 0)),
            pl.BlockSpec((1, D), lambda i, j: (0, 0)),
            pl.BlockSpec((D, tn), lambda i, j: (0, j)),
            pl.BlockSpec((tm, P), lambda i, j: (i, 0)),
            pl.BlockSpec((P, tn), lambda i, j: (0, j)),
            pl.BlockSpec((tm, tn), lambda i, j: (i, j)),
        ],
        out_specs=pl.BlockSpec((tm, tn), lambda i, j: (i, j)),
        scratch_shapes=[pltpu.VMEM((tm, D), BF16)],
        compiler_params=_params("parallel", "arbitrary"),
        name="pe_inject",
    )(h, g.reshape(1, D), w_gate, pe, w_proj, h)


def _rmsnorm_kernel(x_ref, g_ref, o_ref):
    o_ref[...] = _rms_rows(x_ref[...], g_ref[...])


def rms_norm(x, g):
    T, D = x.shape
    tm = _tile(TILE_M_DOWN, T)
    return pl.pallas_call(
        _rmsnorm_kernel,
        out_shape=jax.ShapeDtypeStruct((T, D), F32),
        grid=(T // tm,),
        in_specs=[pl.BlockSpec((tm, D), lambda i: (i, 0)), pl.BlockSpec((1, D), lambda i: (0, 0))],
        out_specs=pl.BlockSpec((tm, D), lambda i: (i, 0)),
        compiler_params=_params("parallel"),
        name="rms_norm",
    )(x, g.reshape(1, D))


def _conv3_rows(x, prev_row, next_row, w):
    tl = x.shape[0]
    rows = lax.broadcasted_iota(jnp.int32, x.shape, 0)
    xm = jnp.where(rows == 0, prev_row, pltpu.roll(x, 1, 0))
    xp = jnp.where(rows == tl - 1, next_row, pltpu.roll(x, tl - 1, 0))
    return xm * w[0:1] + x * w[1:2] + xp * w[2:3]


def _halo_specs(tl, tc, n_l, col_block):
    hb = tl // BF16_SUBLANES
    n_h = n_l * hb
    main = pl.BlockSpec((1, tl, tc), lambda b, l, c: (b, l, col_block(c)))
    prev = pl.BlockSpec((1, BF16_SUBLANES, tc), lambda b, l, c: (b, jnp.maximum(l * hb - 1, 0), col_block(c)))
    nxt = pl.BlockSpec((1, BF16_SUBLANES, tc), lambda b, l, c: (b, jnp.minimum((l + 1) * hb, n_h - 1), col_block(c)))
    return main, prev, nxt


def _edge_rows(prev_ref, next_ref):
    l = pl.program_id(1)
    prev = prev_ref[0].astype(F32)[BF16_SUBLANES - 1:BF16_SUBLANES]
    nxt = next_ref[0].astype(F32)[0:1]
    prev = jnp.where(l == 0, 0.0, prev)
    nxt = jnp.where(l == pl.num_programs(1) - 1, 0.0, nxt)
    return prev, nxt


def _conv_a_kernel(b_ref, c_ref, cp_ref, cn_ref, x_ref, xp_ref, xn_ref, w_ref, o_ref):
    cprev, cnext = _edge_rows(cp_ref, cn_ref)
    xprev, xnext = _edge_rows(xp_ref, xn_ref)
    p = c_ref[0].astype(F32) * x_ref[0].astype(F32)
    y = _conv3_rows(p, cprev * xprev, cnext * xnext, w_ref[...])
    o_ref[0] = (b_ref[0].astype(F32) * y).astype(o_ref.dtype)


def conv_a(proj, w, dc):
    B, L, _ = proj.shape
    tl, tc = _tile(TILE_L, L), _tile(TILE_C, dc)
    nc, nl = dc // tc, L // tl
    b_spec = pl.BlockSpec((1, tl, tc), lambda b, l, c: (b, l, c))
    c_main, c_prev, c_next = _halo_specs(tl, tc, nl, lambda c: nc + c)
    x_main, x_prev, x_next = _halo_specs(tl, tc, nl, lambda c: 2 * nc + c)
    return pl.pallas_call(
        _conv_a_kernel,
        out_shape=jax.ShapeDtypeStruct((B, L, dc), BF16),
        grid=(B, nl, nc),
        in_specs=[b_spec, c_main, c_prev, c_next, x_main, x_prev, x_next,
                  pl.BlockSpec((3, tc), lambda b, l, c: (0, c))],
        out_specs=pl.BlockSpec((1, tl, tc), lambda b, l, c: (b, l, c)),
        compiler_params=_params("parallel", "arbitrary", "arbitrary"),
        name="conv_a",
    )(proj, proj, proj, proj, proj, proj, proj, w)


def _hy_short_kernel(*refs):
    ins, w_refs, outs = refs[:9], refs[9:12], refs[12:]
    for s in range(3):
        x_ref, p_ref, n_ref = ins[3 * s:3 * s + 3]
        prev, nxt = _edge_rows(p_ref, n_ref)
        y = _conv3_rows(x_ref[0].astype(F32), prev, nxt, w_refs[s][...])
        outs[s][0] = y.astype(outs[s].dtype)


def hyena_short_conv(proj, w, col0, dh):
    B, L, _ = proj.shape
    tl, tc = _tile(TILE_L, L), _tile(TILE_C, dh)
    nc, nl = dh // tc, L // tl
    base = col0 // tc
    in_specs, w_specs = [], []
    for s in range(3):
        in_specs += list(_halo_specs(tl, tc, nl, lambda c, s=s: base + s * nc + c))
        w_specs.append(pl.BlockSpec((3, tc), lambda b, l, c, s=s: (0, s * nc + c)))
    out_spec = pl.BlockSpec((1, tl, tc), lambda b, l, c: (b, l, c))
    out = jax.ShapeDtypeStruct((B, L, dh), BF16)
    return pl.pallas_call(
        _hy_short_kernel,
        out_shape=(out, out, out),
        grid=(B, nl, nc),
        in_specs=in_specs + w_specs,
        out_specs=(out_spec, out_spec, out_spec),
        compiler_params=_params("parallel", "arbitrary", "arbitrary"),
        name="hyena_short_conv",
    )(*([proj] * 9), w, w, w)


def _norm_rope(x, g, cos_t, sin_t):
    xn = _rms_rows(x, g)
    lane = lax.broadcasted_iota(jnp.int32, xn.shape, 1)
    quarter = xn.shape[1] // 4
    swapped = jnp.where((lane % (2 * quarter)) < quarter,
                        pltpu.roll(xn, 3 * quarter, 1), pltpu.roll(xn, quarter, 1))
    return xn * cos_t + swapped * sin_t


def _q_prep_kernel(x_ref, g_ref, cos_ref, sin_ref, o_ref, *, scale):
    y = _norm_rope(x_ref[0].astype(F32), g_ref[...], cos_ref[...], sin_ref[...])
    o_ref[0, 0] = (y * scale).astype(o_ref.dtype)


def q_prep(proj, gain, cos_t, sin_t, col0, n_q, hd):
    B, L, _ = proj.shape
    tl = _tile(TILE_L, L)
    base = col0 // hd
    return pl.pallas_call(
        functools.partial(_q_prep_kernel, scale=hd ** -0.5),
        out_shape=jax.ShapeDtypeStruct((B, n_q, L, hd), BF16),
        grid=(B, L // tl, n_q),
        in_specs=[
            pl.BlockSpec((1, tl, hd), lambda b, l, h: (b, l, base + h)),
            pl.BlockSpec((1, hd), lambda b, l, h: (0, 0)),
            pl.BlockSpec((tl, hd), lambda b, l, h: (l, 0)),
            pl.BlockSpec((tl, hd), lambda b, l, h: (l, 0)),
        ],
        out_specs=pl.BlockSpec((1, 1, tl, hd), lambda b, l, h: (b, h, l, 0)),
        compiler_params=_params("parallel", "arbitrary", "arbitrary"),
        name="q_prep",
    )(proj, gain.reshape(1, hd), cos_t, sin_t)


def _kv_prep_kernel(k_ref, v_ref, g_ref, cos_ref, sin_ref, kt_ref, vx_ref):
    y = _norm_rope(k_ref[0].astype(F32), g_ref[...], cos_ref[...], sin_ref[...])
    kt_ref[0, 0, 0] = y.T.astype(kt_ref.dtype)
    v = v_ref[0]
    vx_ref[0, 0] = jnp.concatenate([v, jnp.ones_like(v)], axis=1)


def kv_prep(proj, gain, cos_t, sin_t, k_col0, v_col0, n_kv, hd, tk):
    B, L, _ = proj.shape
    k_base, v_base = k_col0 // hd, v_col0 // hd
    return pl.pallas_call(
        _kv_prep_kernel,
        out_shape=(jax.ShapeDtypeStruct((B, n_kv, L // tk, hd, tk), BF16),
                   jax.ShapeDtypeStruct((B, n_kv, L, 2 * hd), BF16)),
        grid=(B, L // tk, n_kv),
        in_specs=[
            pl.BlockSpec((1, tk, hd), lambda b, l, h: (b, l, k_base + h)),
            pl.BlockSpec((1, tk, hd), lambda b, l, h: (b, l, v_base + h)),
            pl.BlockSpec((1, hd), lambda b, l, h: (0, 0)),
            pl.BlockSpec((tk, hd), lambda b, l, h: (l, 0)),
            pl.BlockSpec((tk, hd), lambda b, l, h: (l, 0)),
        ],
        out_specs=(pl.BlockSpec((1, 1, 1, hd, tk), lambda b, l, h: (b, h, l, 0, 0)),
                   pl.BlockSpec((1, 1, tk, 2 * hd), lambda b, l, h: (b, h, l, 0))),
        compiler_params=_params("parallel", "arbitrary", "arbitrary"),
        name="kv_prep",
    )(proj, proj, gain.reshape(1, hd), cos_t, sin_t)


def _flash_kernel(q_ref, kt_ref, vx_ref, o_ref, s_ref, m_ref, acc_ref):
    group, tq, hd = q_ref.shape[1:]
    nk, _, tk = kt_ref.shape[2:]
    q = q_ref[0].reshape(group * tq, hd)

    def scores(j):
        return jnp.dot(q, kt_ref[0, 0, j], preferred_element_type=F32)

    m_ref[...] = jnp.full(m_ref.shape, -jnp.inf, F32)
    acc_ref[...] = jnp.zeros(acc_ref.shape, F32)
    s_ref[0] = scores(0)

    def step(j, slot):
        s_ref[1 - slot] = scores(jnp.minimum(j + 1, nk - 1))
        s = s_ref[slot]
        m_prev = m_ref[...]
        m_new = jnp.maximum(m_prev, jnp.broadcast_to(jnp.max(s, axis=1, keepdims=True), m_prev.shape))
        alpha = jnp.exp(m_prev - m_new)
        p = jnp.exp(s - jnp.concatenate([m_new] * (tk // LANES), axis=1))
        vj = vx_ref[0, 0, pl.ds(pl.multiple_of(j * tk, tk), tk), :]
        acc_ref[...] = (jnp.concatenate([alpha] * (2 * hd // LANES), axis=1) * acc_ref[...]
                        + jnp.dot(p.astype(BF16), vj, preferred_element_type=F32))
        m_ref[...] = m_new

    def body(i, carry):
        step(2 * i, 0)
        step(2 * i + 1, 1)
        return carry

    lax.fori_loop(0, nk // 2, body, 0)
    acc = acc_ref[...]
    out = acc[:, :hd] / acc[:, hd:]
    for g in range(group):
        o_ref[0, :, g * hd:(g + 1) * hd] = out[g * tq:(g + 1) * tq].astype(o_ref.dtype)


def flash_attention(q, kt, vx):
    B, n_q, L, hd = q.shape
    _, n_kv, nk, _, tk = kt.shape
    assert hd == LANES and nk % 2 == 0
    group = n_q // n_kv
    tq = _tile(TILE_Q, L)
    return pl.pallas_call(
        _flash_kernel,
        out_shape=jax.ShapeDtypeStruct((B, L, n_q * hd), BF16),
        grid=(B, n_kv, L // tq),
        in_specs=[
            pl.BlockSpec((1, group, tq, hd), lambda b, h, i: (b, h, i, 0)),
            pl.BlockSpec((1, 1, nk, hd, tk), lambda b, h, i: (b, h, 0, 0, 0)),
            pl.BlockSpec((1, 1, L, 2 * hd), lambda b, h, i: (b, h, 0, 0)),
        ],
        out_specs=pl.BlockSpec((1, tq, group * hd), lambda b, h, i: (b, i, h)),
        scratch_shapes=[
            pltpu.VMEM((2, group * tq, tk), F32),
            pltpu.VMEM((group * tq, LANES), F32),
            pltpu.VMEM((group * tq, 2 * hd), F32),
        ],
        compiler_params=_params("parallel", "parallel", "arbitrary"),
        name="flash_attention",
    )(q, kt, vx)


def _hyena_filter_kernel(z_ref, w1_ref, b1_ref, f1_ref, w2_ref, b2_ref, f2_ref, w3_ref, dec_ref, bwd_ref,
                         h_ref, l1_ref):
    i = pl.program_id(0)
    hi = lax.Precision.HIGHEST
    z = z_ref[...]
    h = jnp.sin(f1_ref[...] * (jnp.dot(z, w1_ref[...], precision=hi, preferred_element_type=F32) + b1_ref[...]))
    h = jnp.sin(f2_ref[...] * (jnp.dot(h, w2_ref[...], precision=hi, preferred_element_type=F32) + b2_ref[...]))
    h = jnp.dot(h, w3_ref[...], precision=hi, preferred_element_type=F32)
    h = h * jnp.exp(-z[:, 0:1] * jnp.abs(dec_ref[...]))
    rows = lax.broadcasted_iota(jnp.int32, h.shape, 0)
    h = jnp.where((rows == 0) & (i == 0) & (bwd_ref[...] > 0.5), 0.0, h)

    @pl.when(i == 0)
    def _():
        l1_ref[...] = jnp.zeros(l1_ref.shape, F32)

    l1_ref[...] += jnp.sum(jnp.abs(h), axis=0, keepdims=True)
    h_ref[...] = h.astype(h_ref.dtype)


def hyena_filter_taps(zfeat, w1, b1, f1, w2, b2, f2, w3, decay, bwd_mask):
    L, E = zfeat.shape
    H = w2.shape[0]
    C4 = w3.shape[1]
    tt = _tile(TILE_HF, L)
    full = lambda a: pl.BlockSpec(a.shape, lambda i: (0, 0))
    args = (zfeat, w1, b1.reshape(1, H), f1.reshape(1, H), w2, b2.reshape(1, H), f2.reshape(1, H), w3,
            decay.reshape(1, C4), bwd_mask.reshape(1, C4))
    return pl.pallas_call(
        _hyena_filter_kernel,
        out_shape=(jax.ShapeDtypeStruct((L, C4), BF16), jax.ShapeDtypeStruct((1, C4), F32)),
        grid=(L // tt,),
        in_specs=[pl.BlockSpec((tt, E), lambda i: (i, 0))] + [full(a) for a in args[1:]],
        out_specs=(pl.BlockSpec((tt, C4), lambda i: (i, 0)), pl.BlockSpec((1, C4), lambda i: (0, 0))),
        compiler_params=_params("arbitrary"),
        name="hyena_filter_taps",
    )(*args)


def _swap_major(x):
    return pltpu.einshape("abc->bac", x)


def _dft_stage1_kernel(w_ref, z_ref, o_ref):
    nb = z_ref.shape[2]
    k2p = o_ref.shape[1]
    zt = _swap_major(z_ref[0].astype(F32)).astype(BF16)
    w = w_ref[...]
    r = jnp.stack([jnp.dot(w, zt[j], preferred_element_type=F32) for j in range(nb)], axis=0)
    o_ref[0] = _swap_major(r).reshape(k2p, 2, nb, -1).astype(o_ref.dtype)


def dft_stage1(w1, z4):
    B, n2h, n1, C = z4.shape
    k2p = w1.shape[0] // 2
    nb, cb = BF16_SUBLANES, _tile(TILE_FFT_C, C)
    return pl.pallas_call(
        _dft_stage1_kernel,
        out_shape=jax.ShapeDtypeStruct((B, k2p, 2, n1, C), BF16),
        grid=(B, n1 // nb, C // cb),
        in_specs=[pl.BlockSpec(w1.shape, lambda b, i, c: (0, 0)),
                  pl.BlockSpec((1, n2h, nb, cb), lambda b, i, c: (b, 0, i, c))],
        out_specs=pl.BlockSpec((1, k2p, 2, nb, cb), lambda b, i, c: (b, 0, 0, i, c)),
        compiler_params=_params("parallel", "arbitrary", "arbitrary"),
        name="dft_stage1",
    )(w1, z4)


def _filter_spectrum_kernel(g_ref, af_ref, ab_ref, lf_ref, lb_ref, o_ref):
    n1 = af_ref.shape[3]
    inv_l1 = 1.0 / (lf_ref[...] + lb_ref[...])
    for kk in range(af_ref.shape[1]):
        g = g_ref[kk]
        xf = jnp.dot(g, af_ref[0, kk].reshape(2 * n1, -1), preferred_element_type=F32)
        xb = jnp.dot(g, ab_ref[0, kk].reshape(2 * n1, -1), preferred_element_type=F32)
        o_ref[0, kk, 0] = (xf[:n1] + xb[:n1]) * inv_l1
        o_ref[0, kk, 1] = (xf[n1:] - xb[n1:]) * inv_l1


def filter_spectrum(g_tab, a5, l1, n_order, dh):
    _, k2p, _, n1, _ = a5.shape
    kb, cb = _tile(TILE_K2, k2p), _tile(TILE_FFT_C, dh)
    nc = dh // cb
    a_spec = lambda d: pl.BlockSpec((1, kb, 2, n1, cb), lambda k, o, c: (0, k, 0, 0, (2 * o + d) * nc + c))
    l_spec = lambda d: pl.BlockSpec((1, cb), lambda k, o, c: (0, (2 * o + d) * nc + c))
    return pl.pallas_call(
        _filter_spectrum_kernel,
        out_shape=jax.ShapeDtypeStruct((n_order, k2p, 2, n1, dh), F32),
        grid=(k2p // kb, n_order, nc),
        in_specs=[pl.BlockSpec((kb, 2 * n1, 2 * n1), lambda k, o, c: (k, 0, 0)),
                  a_spec(0), a_spec(1), l_spec(0), l_spec(1)],
        out_specs=pl.BlockSpec((1, kb, 2, n1, cb), lambda k, o, c: (o, k, 0, 0, c)),
        compiler_params=_params("parallel", "arbitrary", "arbitrary"),
        name="filter_spectrum",
    )(g_tab, a5, a5, l1, l1)


def _dft_stage2_kernel(g_ref, gt_ref, a_ref, kf_ref, o_ref):
    n1 = a_ref.shape[3]
    for kk in range(a_ref.shape[1]):
        x = jnp.dot(g_ref[kk], a_ref[0, kk].reshape(2 * n1, -1), preferred_element_type=F32)
        xr, xi = x[:n1], x[n1:]
        kr, ki = kf_ref[0, kk, 0], kf_ref[0, kk, 1]
        p = jnp.concatenate([xr * kr - xi * ki, xr * ki + xi * kr], axis=0).astype(BF16)
        q = jnp.dot(gt_ref[kk], p, preferred_element_type=F32)
        o_ref[0, kk] = q.reshape(2, n1, -1).astype(o_ref.dtype)


def dft_stage2(g_tab, gt_tab, a5, kf, order):
    B, k2p, _, n1, C = a5.shape
    kb, cb = _tile(TILE_K2, k2p), _tile(TILE_FFT_C, C)
    g_spec = pl.BlockSpec((kb, 2 * n1, 2 * n1), lambda k, c, b: (k, 0, 0))
    return pl.pallas_call(
        _dft_stage2_kernel,
        out_shape=jax.ShapeDtypeStruct(a5.shape, BF16),
        grid=(k2p // kb, C // cb, B),
        in_specs=[g_spec, g_spec,
                  pl.BlockSpec((1, kb, 2, n1, cb), lambda k, c, b: (b, k, 0, 0, c)),
                  pl.BlockSpec((1, kb, 2, n1, cb), lambda k, c, b: (order, k, 0, 0, c))],
        out_specs=pl.BlockSpec((1, kb, 2, n1, cb), lambda k, c, b: (b, k, 0, 0, c)),
        compiler_params=_params("parallel", "arbitrary", "arbitrary"),
        name="dft_stage2",
    )(g_tab, gt_tab, a5, kf)


def _dft_stage3_kernel(w_ref, q_ref, z_ref, gate_ref, skip_ref, o_ref):
    k2p, _, nb, cb = q_ref.shape[1:]
    qt = _swap_major(q_ref[0].astype(F32).reshape(2 * k2p, nb, cb)).astype(BF16)
    w = w_ref[...]
    y = jnp.stack([jnp.dot(w, qt[j], preferred_element_type=F32) for j in range(nb)], axis=0)
    y = _swap_major(y)
    z = z_ref[0].astype(F32)
    o_ref[0] = (gate_ref[0].astype(F32) * (y + z * skip_ref[...])).astype(o_ref.dtype)


def dft_stage3(wi, q5, z4, gate4, skip):
    B, k2p, _, n1, C = q5.shape
    n2h = wi.shape[0]
    nb, cb = BF16_SUBLANES, _tile(TILE_FFT_C, C)
    blk = pl.BlockSpec((1, n2h, nb, cb), lambda b, i, c: (b, 0, i, c))
    return pl.pallas_call(
        _dft_stage3_kernel,
        out_shape=jax.ShapeDtypeStruct((B, n2h, n1, C), BF16),
        grid=(B, n1 // nb, C // cb),
        in_specs=[pl.BlockSpec(wi.shape, lambda b, i, c: (0, 0)),
                  pl.BlockSpec((1, k2p, 2, nb, cb), lambda b, i, c: (b, 0, 0, i, c)),
                  blk, blk, pl.BlockSpec((1, cb), lambda b, i, c: (0, c))],
        out_specs=blk,
        compiler_params=_params("parallel", "arbitrary", "arbitrary"),
        name="dft_stage3",
    )(wi, q5, z4, gate4, skip.reshape(1, C))


def _dft_tables(L):
    n1 = FFT_N1
    n2h = L // n1
    n2 = 2 * n2h
    n = 2 * L
    k2 = n2h + 1
    k2p = -(-k2 // 8) * 8
    kk = np.arange(k2)
    ang1 = 2.0 * np.pi * np.outer(kk, np.arange(n2h)) / n2
    w1 = np.zeros((k2p, 2, n2h))
    w1[:k2, 0], w1[:k2, 1] = np.cos(ang1), -np.sin(ang1)
    c = np.where((kk == 0) | (kk == n2h), 1.0, 2.0)
    wi = np.zeros((n2h, k2p, 2))
    wi[:, :k2, 0], wi[:, :k2, 1] = (c[:, None] * np.cos(ang1)).T / n, (-c[:, None] * np.sin(ang1)).T / n
    freq = kk[:, None, None] + n2 * np.arange(n1)[None, :, None]
    ang2 = 2.0 * np.pi * freq * np.arange(n1)[None, None, :] / n
    gr, gi = np.cos(ang2), -np.sin(ang2)
    g = np.zeros((k2p, 2 * n1, 2 * n1))
    g[:k2] = np.concatenate([np.concatenate([gr, -gi], axis=2), np.concatenate([gi, gr], axis=2)], axis=1)
    to = lambda a: jnp.asarray(a.astype(np.float32)).astype(BF16)
    return to(w1.reshape(2 * k2p, n2h)), to(wi.reshape(n2h, 2 * k2p)), to(g), to(g.transpose(0, 2, 1)), k2p


def _rope_tables(L, hd):
    quarter = hd // 4
    rows_n = L // GRID_W
    row = jnp.repeat(jnp.arange(rows_n, dtype=F32), GRID_W)
    col = jnp.tile(jnp.arange(GRID_W, dtype=F32), rows_n)
    half = hd // 2
    inv = ROPE_THETA ** (-jnp.arange(0, half, 2, dtype=F32) / half)
    ar, ac = row[:, None] * inv[None, :], col[:, None] * inv[None, :]
    cos_t = jnp.concatenate([jnp.cos(ar), jnp.cos(ar), jnp.cos(ac), jnp.cos(ac)], axis=-1)
    sin_t = jnp.concatenate([-jnp.sin(ar), jnp.sin(ar), -jnp.sin(ac), jnp.sin(ac)], axis=-1)
    assert cos_t.shape == (L, 4 * quarter)
    return cos_t, sin_t


def _hyena_pos_features(L, emb):
    bands = (emb - 1) // 2
    t = jnp.linspace(0.0, 1.0, L, dtype=F32)[:, None]
    w = 2.0 * math.pi * jnp.arange(L, dtype=F32)[:, None] / L
    f = jnp.linspace(1e-4, bands - 1, bands, dtype=F32)[None, :]
    ang = w * f
    return jnp.concatenate([t, jnp.cos(ang), -jnp.sin(ang)], axis=-1)


def kernel(x_prompt, x_sample, p_prompt, p_sample, n_ffn1, w_ffn1_up, w_ffn1_down, n_mix, w_in, w_gate, conv_a_w, hy_short_w, hf_w1, hf_b1, hf_freq1, hf_w2, hf_b2, hf_freq2, hf_w3, hf_decay, hf_skip, q_norm, k_norm, w_pa, w_pb, w_pc, w_o, n_ffn2, w_ffn2_up, w_ffn2_down, n_pe, w_pe_gate, w_pe_proj, final_norm):
    assert x_prompt.shape[1:] == x_sample.shape[1:]
    bp, L, D = x_prompt.shape
    B = bp + x_sample.shape[0]
    T = B * L
    depth = w_in.shape[0]
    dc = conv_a_w.shape[-1]
    n_order, dh = hf_skip.shape[1:]
    hd = q_norm.shape[-1]
    qw = w_pc.shape[1]
    kvw = (w_in.shape[-1] - 3 * dc - 3 * dh - qw) // 2
    n_q, n_kv = qw // hd, kvw // hd
    emb = hf_w1.shape[1]
    assert n_order == 2 and hf_w3.shape[-1] == 4 * dh and L % (FFT_N1 * 8) == 0 and L % GRID_W == 0

    h = jnp.concatenate([x_prompt, x_sample], axis=0).reshape(T, D)
    pe_all = jnp.concatenate([p_prompt, p_sample], axis=1).astype(BF16).reshape(depth, T, -1)

    w1_tab, wi_tab, g_tab, gt_tab, k2p = _dft_tables(L)
    cos_t, sin_t = _rope_tables(L, hd)
    emb_p = -(-emb // 8) * 8
    zfeat = jnp.pad(_hyena_pos_features(L, emb), ((0, 0), (0, emb_p - emb)))
    bwd_mask = jnp.tile(jnp.concatenate([jnp.zeros((dh,), F32), jnp.ones((dh,), F32)]), n_order)
    n2h = L // FFT_N1

    def long_conv(z, gate, kf, order, skip):
        z4 = z.reshape(B, n2h, FFT_N1, dh)
        a = dft_stage1(w1_tab, z4)
        q = dft_stage2(g_tab, gt_tab, a, kf, order)
        return dft_stage3(wi_tab, q, z4, gate.reshape(B, n2h, FFT_N1, dh), skip).reshape(B, L, dh)

    for i in range(depth):
        bf = lambda w: w[i].astype(BF16)
        act = norm_swiglu_up(h, n_ffn1[i], bf(w_ffn1_up))
        h = matmul_residual(act, bf(w_ffn1_down), h, 0.5)
        proj = norm_matmul(h, n_mix[i], bf(w_in)).reshape(B, L, -1)
        gates = norm_matmul(h, n_mix[i], bf(w_gate), act="sigmoid")
        ya = conv_a(proj, conv_a_w[i], dc)
        hv, hx1, hx2 = hyena_short_conv(proj, hy_short_w[i], 3 * dc, dh)
        w1p = jnp.pad(hf_w1[i], ((0, emb_p - emb), (0, 0)))
        taps, l1 = hyena_filter_taps(zfeat, w1p, hf_b1[i], hf_freq1[i], hf_w2[i], hf_b2[i], hf_freq2[i],
                                     hf_w3[i], hf_decay[i], bwd_mask)
        af = dft_stage1(w1_tab, taps.reshape(1, n2h, FFT_N1, 4 * dh))
        kf = filter_spectrum(g_tab, af, l1, n_order, dh)
        zb = long_conv(hv, hx1, kf, 0, hf_skip[i, 0])
        zb = long_conv(zb, hx2, kf, 1, hf_skip[i, 1])
        q_col0 = 3 * dc + 3 * dh
        qr = q_prep(proj, q_norm[i], cos_t, sin_t, q_col0, n_q, hd)
        kt, vx = kv_prep(proj, k_norm[i], cos_t, sin_t, q_col0 + qw, q_col0 + qw + kvw, n_kv, hd,
                         _tile(TILE_K, L))
        yc = flash_attention(qr, kt, vx)
        m = gated_merge(ya.reshape(T, dc), zb.reshape(T, dh), yc.reshape(T, qw), gates,
                        bf(w_pa), bf(w_pb), bf(w_pc))
        h = matmul_residual(m, bf(w_o), h, 1.0)
        act = norm_swiglu_up(h, n_ffn2[i], bf(w_ffn2_up))
        h = matmul_residual(act, bf(w_ffn2_down), h, 0.5)
        h = pe_inject(h, n_pe[i], bf(w_pe_gate), pe_all[i], bf(w_pe_proj))

    y = rms_norm(h, final_norm).reshape(B, L, D)
    return (y[:bp], y[bp:])
```

```python
import functools
import math

import numpy as np
import jax
import jax.numpy as jnp
from jax import lax
from jax.experimental import pallas as pl
from jax.experimental.pallas import tpu as pltpu

F32 = jnp.float32
BF16 = jnp.bfloat16

RMS_EPS = 1e-6
ROPE_THETA = 10000.0
GRID_W = 64
LANES = 128
BF16_SUBLANES = 16
FFT_N1 = 128
VMEM_LIMIT_BYTES = 52 * 1024 * 1024

TILE_M = 1024
TILE_M_DOWN = 1024
TILE_N = 512
TILE_N_WIDE = 1536
TILE_K_SHORT = 2048
TILE_L = 1024
TILE_C = 512
TILE_Q = 512
TILE_K = 1024
TILE_K2 = 8
TILE_FFT_C = 512
TILE_HF = 256


def _tile(default, dim):
    t = min(default, dim)
    assert dim % t == 0, (default, dim)
    return t


def _params(*sem):
    return pltpu.CompilerParams(dimension_semantics=sem, vmem_limit_bytes=VMEM_LIMIT_BYTES)


def _rms_rows(x, g):
    ms = jnp.mean(x * x, axis=-1, keepdims=True)
    return x * lax.rsqrt(ms + RMS_EPS) * g


def _norm_mm_kernel(x_ref, g_ref, w_ref, o_ref, xn_ref, *, act):
    @pl.when(pl.program_id(1) == 0)
    def _():
        xn_ref[...] = _rms_rows(x_ref[...], g_ref[...]).astype(BF16)

    y = jnp.dot(xn_ref[...], w_ref[...], preferred_element_type=F32)
    if act == "sigmoid":
        y = jax.nn.sigmoid(y)
    o_ref[...] = y.astype(o_ref.dtype)


def norm_matmul(x, g, w, act=None):
    T, D = x.shape
    N = w.shape[1]
    tm = _tile(TILE_M, T)
    tn = TILE_N_WIDE if N % TILE_N_WIDE == 0 else _tile(TILE_N, N)
    return pl.pallas_call(
        functools.partial(_norm_mm_kernel, act=act),
        out_shape=jax.ShapeDtypeStruct((T, N), BF16),
        grid=(T // tm, N // tn),
        in_specs=[
            pl.BlockSpec((tm, D), lambda i, j: (i, 0)),
            pl.BlockSpec((1, D), lambda i, j: (0, 0)),
            pl.BlockSpec((D, tn), lambda i, j: (0, j)),
        ],
        out_specs=pl.BlockSpec((tm, tn), lambda i, j: (i, j)),
        scratch_shapes=[pltpu.VMEM((tm, D), BF16)],
        compiler_params=_params("parallel", "arbitrary"),
        name="norm_matmul",
    )(x, g.reshape(1, D), w)


def _norm_swiglu_kernel(x_ref, g_ref, wa_ref, wb_ref, o_ref, xn_ref):
    @pl.when(pl.program_id(1) == 0)
    def _():
        xn_ref[...] = _rms_rows(x_ref[...], g_ref[...]).astype(BF16)

    xn = xn_ref[...]
    a = jnp.dot(xn, wa_ref[...], preferred_element_type=F32)
    b = jnp.dot(xn, wb_ref[...], preferred_element_type=F32)
    o_ref[...] = (a * jax.nn.sigmoid(a) * b).astype(o_ref.dtype)


def norm_swiglu_up(x, g, w_up):
    T, D = x.shape
    F = w_up.shape[1] // 2
    tm, tn = _tile(TILE_M, T), _tile(TILE_N, F)
    nb = F // tn
    return pl.pallas_call(
        _norm_swiglu_kernel,
        out_shape=jax.ShapeDtypeStruct((T, F), BF16),
        grid=(T // tm, nb),
        in_specs=[
            pl.BlockSpec((tm, D), lambda i, j: (i, 0)),
            pl.BlockSpec((1, D), lambda i, j: (0, 0)),
            pl.BlockSpec((D, tn), lambda i, j: (0, j)),
            pl.BlockSpec((D, tn), lambda i, j: (0, nb + j)),
        ],
        out_specs=pl.BlockSpec((tm, tn), lambda i, j: (i, j)),
        scratch_shapes=[pltpu.VMEM((tm, D), BF16)],
        compiler_params=_params("parallel", "arbitrary"),
        name="norm_swiglu_up",
    )(x, g.reshape(1, D), w_up, w_up)


def _mm_residual_kernel(a_ref, w_ref, r_ref, o_ref, *, scale):
    y = jnp.dot(a_ref[...], w_ref[...], preferred_element_type=F32)
    o_ref[...] = r_ref[...] + scale * y


def matmul_residual(a, w, res, scale):
    T, K = a.shape
    N = w.shape[1]
    tm, tn = _tile(TILE_M_DOWN, T), _tile(TILE_N if K > TILE_K_SHORT else 2 * TILE_N, N)
    return pl.pallas_call(
        functools.partial(_mm_residual_kernel, scale=scale),
        out_shape=jax.ShapeDtypeStruct((T, N), F32),
        grid=(T // tm, N // tn),
        in_specs=[
            pl.BlockSpec((tm, K), lambda i, j: (i, 0)),
            pl.BlockSpec((K, tn), lambda i, j: (0, j)),
            pl.BlockSpec((tm, tn), lambda i, j: (i, j)),
        ],
        out_specs=pl.BlockSpec((tm, tn), lambda i, j: (i, j)),
        compiler_params=_params("parallel", "arbitrary"),
        name="matmul_residual",
    )(a, w, res)


def _merge_kernel(ya_ref, zb_ref, yc_ref, ga_ref, gb_ref, gc_ref, wa_ref, wb_ref, wc_ref, o_ref):
    pa = jnp.dot(ya_ref[...], wa_ref[...], preferred_element_type=F32)
    pb = jnp.dot(zb_ref[...], wb_ref[...], preferred_element_type=F32)
    pc = jnp.dot(yc_ref[...], wc_ref[...], preferred_element_type=F32)
    m = ga_ref[...].astype(F32) * pa + gb_ref[...].astype(F32) * pb + gc_ref[...].astype(F32) * pc
    o_ref[...] = m.astype(o_ref.dtype)


def gated_merge(ya, zb, yc, gates, w_pa, w_pb, w_pc):
    T = ya.shape[0]
    D = w_pa.shape[1]
    tm, tn = _tile(TILE_M, T), _tile(TILE_N, D)
    nb = D // tn
    row = lambda a: pl.BlockSpec((tm, a.shape[1]), lambda i, j: (i, 0))
    col = lambda w: pl.BlockSpec((w.shape[0], tn), lambda i, j: (0, j))
    gate = lambda b: pl.BlockSpec((tm, tn), lambda i, j: (i, b * nb + j))
    return pl.pallas_call(
        _merge_kernel,
        out_shape=jax.ShapeDtypeStruct((T, D), BF16),
        grid=(T // tm, nb),
        in_specs=[row(ya), row(zb), row(yc), gate(0), gate(1), gate(2), col(w_pa), col(w_pb), col(w_pc)],
        out_specs=pl.BlockSpec((tm, tn), lambda i, j: (i, j)),
        compiler_params=_params("parallel", "arbitrary"),
        name="gated_merge",
    )(ya, zb, yc, gates, gates, gates, w_pa, w_pb, w_pc)


def _pe_kernel(x_ref, g_ref, wg_ref, pe_ref, wp_ref, r_ref, o_ref, xn_ref):
    @pl.when(pl.program_id(1) == 0)
    def _():
        xn_ref[...] = _rms_rows(x_ref[...], g_ref[...]).astype(BF16)

    gate = jax.nn.sigmoid(jnp.dot(xn_ref[...], wg_ref[...], preferred_element_type=F32))
    emb = jnp.dot(pe_ref[...], wp_ref[...], preferred_element_type=F32)
    o_ref[...] = r_ref[...] + gate * emb


def pe_inject(h, g, w_gate, pe, w_proj):
    T, D = h.shape
    P = pe.shape[1]
    tm, tn = _tile(TILE_M, T), _tile(TILE_N, D)
    return pl.pallas_call(
        _pe_kernel,
        out_shape=jax.ShapeDtypeStruct((T, D), F32),
        grid=(T // tm, D // tn),
        in_specs=[
            pl.BlockSpec((tm, D), lambda i, j: (i, 0)),
            pl.BlockSpec((1, D), lambda i, j: (0, 0)),
            pl.BlockSpec((D, tn), lambda i, j: (0, j)),
            pl.BlockSpec((tm, P), lambda i, j: (i, 0)),
            pl.BlockSpec((P, tn), lambda i, j: (0, j)),
            pl.BlockSpec((tm, tn), lambda i, j: (i, j)),
        ],
        out_specs=pl.BlockSpec((tm, tn), lambda i, j: (i, j)),
        scratch_shapes=[pltpu.VMEM((tm, D), BF16)],
        compiler_params=_params("parallel", "arbitrary"),
        name="pe_inject",
    )(h, g.reshape(1, D), w_gate, pe, w_proj, h)


def _rmsnorm_kernel(x_ref, g_ref, o_ref):
    o_ref[...] = _rms_rows(x_ref[...], g_ref[...])


def rms_norm(x, g):
    T, D = x.shape
    tm = _tile(TILE_M_DOWN, T)
    return pl.pallas_call(
        _rmsnorm_kernel,
        out_shape=jax.ShapeDtypeStruct((T, D), F32),
        grid=(T // tm,),
        in_specs=[pl.BlockSpec((tm, D), lambda i: (i, 0)), pl.BlockSpec((1, D), lambda i: (0, 0))],
        out_specs=pl.BlockSpec((tm, D), lambda i: (i, 0)),
        compiler_params=_params("parallel"),
        name="rms_norm",
    )(x, g.reshape(1, D))


def _conv3_rows(x, prev_row, next_row, w):
    tl = x.shape[0]
    rows = lax.broadcasted_iota(jnp.int32, x.shape, 0)
    xm = jnp.where(rows == 0, prev_row, pltpu.roll(x, 1, 0))
    xp = jnp.where(rows == tl - 1, next_row, pltpu.roll(x, tl - 1, 0))
    return xm * w[0:1] + x * w[1:2] + xp * w[2:3]


def _halo_specs(tl, tc, n_l, col_block):
    hb = tl // BF16_SUBLANES
    n_h = n_l * hb
    main = pl.BlockSpec((1, tl, tc), lambda b, l, c: (b, l, col_block(c)))
    prev = pl.BlockSpec((1, BF16_SUBLANES, tc), lambda b, l, c: (b, jnp.maximum(l * hb - 1, 0), col_block(c)))
    nxt = pl.BlockSpec((1, BF16_SUBLANES, tc), lambda b, l, c: (b, jnp.minimum((l + 1) * hb, n_h - 1), col_block(c)))
    return main, prev, nxt


def _edge_rows(prev_ref, next_ref):
    l = pl.program_id(1)
    prev = prev_ref[0].astype(F32)[BF16_SUBLANES - 1:BF16_SUBLANES]
    nxt = next_ref[0].astype(F32)[0:1]
    prev = jnp.where(l == 0, 0.0, prev)
    nxt = jnp.where(l == pl.num_programs(1) - 1, 0.0, nxt)
    return prev, nxt


def _conv_a_kernel(b_ref, c_ref, cp_ref, cn_ref, x_ref, xp_ref, xn_ref, w_ref, o_ref):
    cprev, cnext = _edge_rows(cp_ref, cn_ref)
    xprev, xnext = _edge_rows(xp_ref, xn_ref)
    p = c_ref[0].astype(F32) * x_ref[0].astype(F32)
    y = _conv3_rows(p, cprev * xprev, cnext * xnext, w_ref[...])
    o_ref[0] = (b_ref[0].astype(F32) * y).astype(o_ref.dtype)


def conv_a(proj, w, dc):
    B, L, _ = proj.shape
    tl, tc = _tile(TILE_L, L), _tile(TILE_C, dc)
    nc, nl = dc // tc, L // tl
    b_spec = pl.BlockSpec((1, tl, tc), lambda b, l, c: (b, l, c))
    c_main, c_prev, c_next = _halo_specs(tl, tc, nl, lambda c: nc + c)
    x_main, x_prev, x_next = _halo_specs(tl, tc, nl, lambda c: 2 * nc + c)
    return pl.pallas_call(
        _conv_a_kernel,
        out_shape=jax.ShapeDtypeStruct((B, L, dc), BF16),
        grid=(B, nl, nc),
        in_specs=[b_spec, c_main, c_prev, c_next, x_main, x_prev, x_next,
                  pl.BlockSpec((3, tc), lambda b, l, c: (0, c))],
        out_specs=pl.BlockSpec((1, tl, tc), lambda b, l, c: (b, l, c)),
        compiler_params=_params("parallel", "arbitrary", "arbitrary"),
        name="conv_a",
    )(proj, proj, proj, proj, proj, proj, proj, w)


def _hy_short_kernel(*refs):
    ins, w_refs, outs = refs[:9], refs[9:12], refs[12:]
    for s in range(3):
        x_ref, p_ref, n_ref = ins[3 * s:3 * s + 3]
        prev, nxt = _edge_rows(p_ref, n_ref)
        y = _conv3_rows(x_ref[0].astype(F32), prev, nxt, w_refs[s][...])
        outs[s][0] = y.astype(outs[s].dtype)


def hyena_short_conv(proj, w, col0, dh):
    B, L, _ = proj.shape
    tl, tc = _tile(TILE_L, L), _tile(TILE_C, dh)
    nc, nl = dh // tc, L // tl
    base = col0 // tc
    in_specs, w_specs = [], []
    for s in range(3):
        in_specs += list(_halo_specs(tl, tc, nl, lambda c, s=s: base + s * nc + c))
        w_specs.append(pl.BlockSpec((3, tc), lambda b, l, c, s=s: (0, s * nc + c)))
    out_spec = pl.BlockSpec((1, tl, tc), lambda b, l, c: (b, l, c))
    out = jax.ShapeDtypeStruct((B, L, dh), BF16)
    return pl.pallas_call(
        _hy_short_kernel,
        out_shape=(out, out, out),
        grid=(B, nl, nc),
        in_specs=in_specs + w_specs,
        out_specs=(out_spec, out_spec, out_spec),
        compiler_params=_params("parallel", "arbitrary", "arbitrary"),
        name="hyena_short_conv",
    )(*([proj] * 9), w, w, w)


def _norm_rope(x, g, cos_t, sin_t):
    xn = _rms_rows(x, g)
    lane = lax.broadcasted_iota(jnp.int32, xn.shape, 1)
    quarter = xn.shape[1] // 4
    swapped = jnp.where((lane % (2 * quarter)) < quarter,
                        pltpu.roll(xn, 3 * quarter, 1), pltpu.roll(xn, quarter, 1))
    return xn * cos_t + swapped * sin_t


def _q_prep_kernel(x_ref, g_ref, cos_ref, sin_ref, o_ref, *, scale):
    hd = o_ref.shape[3]
    for h in range(o_ref.shape[1]):
        y = _norm_rope(x_ref[0, :, h * hd:(h + 1) * hd].astype(F32), g_ref[...], cos_ref[...], sin_ref[...])
        o_ref[0, h] = (y * scale).astype(o_ref.dtype)


def q_prep(proj, gain, cos_t, sin_t, col0, n_q, hd):
    B, L, _ = proj.shape
    tl = _tile(TILE_L, L)
    hb = max(d for d in range(1, n_q + 1) if n_q % d == 0 and col0 % (d * hd) == 0)
    base = col0 // (hb * hd)
    return pl.pallas_call(
        functools.partial(_q_prep_kernel, scale=hd ** -0.5),
        out_shape=jax.ShapeDtypeStruct((B, n_q, L, hd), BF16),
        grid=(B, L // tl, n_q // hb),
        in_specs=[
            pl.BlockSpec((1, tl, hb * hd), lambda b, l, h: (b, l, base + h)),
            pl.BlockSpec((1, hd), lambda b, l, h: (0, 0)),
            pl.BlockSpec((tl, hd), lambda b, l, h: (l, 0)),
            pl.BlockSpec((tl, hd), lambda b, l, h: (l, 0)),
        ],
        out_specs=pl.BlockSpec((1, hb, tl, hd), lambda b, l, h: (b, h, l, 0)),
        compiler_params=_params("parallel", "arbitrary", "arbitrary"),
        name="q_prep",
    )(proj, gain.reshape(1, hd), cos_t, sin_t)


def _kv_prep_kernel(k_ref, v_ref, g_ref, cos_ref, sin_ref, kt_ref, vx_ref):
    y = _norm_rope(k_ref[0].astype(F32), g_ref[...], cos_ref[...], sin_ref[...])
    kt_ref[0, 0, 0] = y.T.astype(kt_ref.dtype)
    v = v_ref[0]
    vx_ref[0, 0] = jnp.concatenate([v, jnp.ones_like(v)], axis=1)


def kv_prep(proj, gain, cos_t, sin_t, k_col0, v_col0, n_kv, hd, tk):
    B, L, _ = proj.shape
    k_base, v_base = k_col0 // hd, v_col0 // hd
    return pl.pallas_call(
        _kv_prep_kernel,
        out_shape=(jax.ShapeDtypeStruct((B, n_kv, L // tk, hd, tk), BF16),
                   jax.ShapeDtypeStruct((B, n_kv, L, 2 * hd), BF16)),
        grid=(B, L // tk, n_kv),
        in_specs=[
            pl.BlockSpec((1, tk, hd), lambda b, l, h: (b, l, k_base + h)),
            pl.BlockSpec((1, tk, hd), lambda b, l, h: (b, l, v_base + h)),
            pl.BlockSpec((1, hd), lambda b, l, h: (0, 0)),
            pl.BlockSpec((tk, hd), lambda b, l, h: (l, 0)),
            pl.BlockSpec((tk, hd), lambda b, l, h: (l, 0)),
        ],
        out_specs=(pl.BlockSpec((1, 1, 1, hd, tk), lambda b, l, h: (b, h, l, 0, 0)),
                   pl.BlockSpec((1, 1, tk, 2 * hd), lambda b, l, h: (b, h, l, 0))),
        compiler_params=_params("parallel", "arbitrary", "arbitrary"),
        name="kv_prep",
    )(proj, proj, gain.reshape(1, hd), cos_t, sin_t)


def _flash_kernel(q_ref, kt_ref, vx_ref, o_ref, s_ref, m_ref, acc_ref):
    group, tq, hd = q_ref.shape[1:]
    nk, _, tk = kt_ref.shape[2:]
    q = q_ref[0].reshape(group * tq, hd)

    def scores(j):
        return jnp.dot(q, kt_ref[0, 0, j], preferred_element_type=F32)

    m_ref[...] = jnp.full(m_ref.shape, -jnp.inf, F32)
    acc_ref[...] = jnp.zeros(acc_ref.shape, F32)
    s_ref[0] = scores(0)

    def step(j, slot):
        s_ref[1 - slot] = scores(jnp.minimum(j + 1, nk - 1))
        s = s_ref[slot]
        m_prev = m_ref[...]
        m_new = jnp.maximum(m_prev, jnp.broadcast_to(jnp.max(s, axis=1, keepdims=True), m_prev.shape))
        alpha = jnp.exp(m_prev - m_new)
        p = jnp.exp(s - jnp.concatenate([m_new] * (tk // LANES), axis=1))
        vj = vx_ref[0, 0, pl.ds(pl.multiple_of(j * tk, tk), tk), :]
        acc_ref[...] = (jnp.concatenate([alpha] * (2 * hd // LANES), axis=1) * acc_ref[...]
                        + jnp.dot(p.astype(BF16), vj, preferred_element_type=F32))
        m_ref[...] = m_new

    def body(i, carry):
        step(2 * i, 0)
        step(2 * i + 1, 1)
        return carry

    lax.fori_loop(0, nk // 2, body, 0)
    acc = acc_ref[...]
    out = acc[:, :hd] / acc[:, hd:]
    for g in range(group):
        o_ref[0, :, g * hd:(g + 1) * hd] = out[g * tq:(g + 1) * tq].astype(o_ref.dtype)


def flash_attention(q, kt, vx):
    B, n_q, L, hd = q.shape
    _, n_kv, nk, _, tk = kt.shape
    assert hd == LANES and nk % 2 == 0
    group = n_q // n_kv
    tq = _tile(TILE_Q, L)
    return pl.pallas_call(
        _flash_kernel,
        out_shape=jax.ShapeDtypeStruct((B, L, n_q * hd), BF16),
        grid=(B, n_kv, L // tq),
        in_specs=[
            pl.BlockSpec((1, group, tq, hd), lambda b, h, i: (b, h, i, 0)),
            pl.BlockSpec((1, 1, nk, hd, tk), lambda b, h, i: (b, h, 0, 0, 0)),
            pl.BlockSpec((1, 1, L, 2 * hd), lambda b, h, i: (b, h, 0, 0)),
        ],
        out_specs=pl.BlockSpec((1, tq, group * hd), lambda b, h, i: (b, i, h)),
        scratch_shapes=[
            pltpu.VMEM((2, group * tq, tk), F32),
            pltpu.VMEM((group * tq, LANES), F32),
            pltpu.VMEM((group * tq, 2 * hd), F32),
        ],
        compiler_params=_params("parallel", "parallel", "arbitrary"),
        name="flash_attention",
    )(q, kt, vx)


def _bf16_pieces(x):
    p0 = x.astype(BF16).astype(F32)
    r = x - p0
    p1 = r.astype(BF16).astype(F32)
    p2 = (r - p1).astype(BF16).astype(F32)
    return p0, p1, p2


def _hyena_filter_kernel(z_ref, w1_ref, b1_ref, f1_ref, w2_ref, b2_ref, f2_ref, w3_ref, dec_ref, bwd_ref,
                         h_ref, l1_ref):
    i = pl.program_id(0)
    hi = lax.Precision.HIGHEST
    z = z_ref[...]
    h = jnp.sin(f1_ref[...] * (jnp.dot(z, w1_ref[...], precision=hi, preferred_element_type=F32) + b1_ref[...]))
    h = jnp.sin(f2_ref[...] * (jnp.dot(h, w2_ref[...], precision=hi, preferred_element_type=F32) + b2_ref[...]))
    h0, h1, h2 = _bf16_pieces(h)
    lhs = jnp.concatenate([h0, h0, h1, h0, h2, h1], axis=1).astype(BF16)
    h = jnp.dot(lhs, w3_ref[...], preferred_element_type=F32)
    h = h * jnp.exp(-z[:, 0:1] * jnp.abs(dec_ref[...]))
    rows = lax.broadcasted_iota(jnp.int32, h.shape, 0)
    h = jnp.where((rows == 0) & (i == 0) & (bwd_ref[...] > 0.5), 0.0, h)

    @pl.when(i == 0)
    def _():
        l1_ref[...] = jnp.zeros(l1_ref.shape, F32)

    l1_ref[...] += jnp.sum(jnp.abs(h), axis=0, keepdims=True)
    h_ref[...] = h.astype(h_ref.dtype)


def hyena_filter_taps(zfeat, w1, b1, f1, w2, b2, f2, w3, decay, bwd_mask):
    L, E = zfeat.shape
    H = w2.shape[0]
    C4 = w3.shape[1]
    tt = _tile(TILE_HF, L)
    full = lambda a: pl.BlockSpec(a.shape, lambda i: (0, 0))
    args = (zfeat, w1, b1.reshape(1, H), f1.reshape(1, H), w2, b2.reshape(1, H), f2.reshape(1, H), w3,
            decay.reshape(1, C4), bwd_mask.reshape(1, C4))
    return pl.pallas_call(
        _hyena_filter_kernel,
        out_shape=(jax.ShapeDtypeStruct((L, C4), BF16), jax.ShapeDtypeStruct((1, C4), F32)),
        grid=(L // tt,),
        in_specs=[pl.BlockSpec((tt, E), lambda i: (i, 0))] + [full(a) for a in args[1:]],
        out_specs=(pl.BlockSpec((tt, C4), lambda i: (i, 0)), pl.BlockSpec((1, C4), lambda i: (0, 0))),
        compiler_params=_params("arbitrary"),
        name="hyena_filter_taps",
    )(*args)


def _swap_major(x):
    return pltpu.einshape("abc->bac", x)


def _dft_stage1_kernel(w_ref, z_ref, o_ref):
    nb = z_ref.shape[2]
    k2p = o_ref.shape[1]
    zt = _swap_major(z_ref[0].astype(F32)).astype(BF16)
    w = w_ref[...]
    r = jnp.stack([jnp.dot(w, zt[j], preferred_element_type=F32) for j in range(nb)], axis=0)
    o_ref[0] = _swap_major(r).reshape(k2p, 2, nb, -1).astype(o_ref.dtype)


def dft_stage1(w1, z4):
    B, n2h, n1, C = z4.shape
    k2p = w1.shape[0] // 2
    nb, cb = BF16_SUBLANES, _tile(TILE_FFT_C, C)
    return pl.pallas_call(
        _dft_stage1_kernel,
        out_shape=jax.ShapeDtypeStruct((B, k2p, 2, n1, C), BF16),
        grid=(B, n1 // nb, C // cb),
        in_specs=[pl.BlockSpec(w1.shape, lambda b, i, c: (0, 0)),
                  pl.BlockSpec((1, n2h, nb, cb), lambda b, i, c: (b, 0, i, c))],
        out_specs=pl.BlockSpec((1, k2p, 2, nb, cb), lambda b, i, c: (b, 0, 0, i, c)),
        compiler_params=_params("parallel", "arbitrary", "arbitrary"),
        name="dft_stage1",
    )(w1, z4)


def _filter_spectrum_kernel(g_ref, af_ref, ab_ref, lf_ref, lb_ref, o_ref):
    n1 = af_ref.shape[3]
    inv_l1 = 1.0 / (lf_ref[...] + lb_ref[...])
    for kk in range(af_ref.shape[1]):
        g = g_ref[kk]
        xf = jnp.dot(g, af_ref[0, kk].reshape(2 * n1, -1), preferred_element_type=F32)
        xb = jnp.dot(g, ab_ref[0, kk].reshape(2 * n1, -1), preferred_element_type=F32)
        o_ref[0, kk, 0] = (xf[:n1] + xb[:n1]) * inv_l1
        o_ref[0, kk, 1] = (xf[n1:] - xb[n1:]) * inv_l1


def filter_spectrum(g_tab, a5, l1, n_order, dh):
    _, k2p, _, n1, _ = a5.shape
    kb, cb = _tile(TILE_K2, k2p), _tile(TILE_FFT_C, dh)
    nc = dh // cb
    a_spec = lambda d: pl.BlockSpec((1, kb, 2, n1, cb), lambda k, o, c: (0, k, 0, 0, (2 * o + d) * nc + c))
    l_spec = lambda d: pl.BlockSpec((1, cb), lambda k, o, c: (0, (2 * o + d) * nc + c))
    return pl.pallas_call(
        _filter_spectrum_kernel,
        out_shape=jax.ShapeDtypeStruct((n_order, k2p, 2, n1, dh), F32),
        grid=(k2p // kb, n_order, nc),
        in_specs=[pl.BlockSpec((kb, 2 * n1, 2 * n1), lambda k, o, c: (k, 0, 0)),
                  a_spec(0), a_spec(1), l_spec(0), l_spec(1)],
        out_specs=pl.BlockSpec((1, kb, 2, n1, cb), lambda k, o, c: (o, k, 0, 0, c)),
        compiler_params=_params("parallel", "arbitrary", "arbitrary"),
        name="filter_spectrum",
    )(g_tab, a5, a5, l1, l1)


def _dft_stage2_kernel(g_ref, gt_ref, a_ref, kf_ref, o_ref):
    n1 = a_ref.shape[3]
    for kk in range(a_ref.shape[1]):
        x = jnp.dot(g_ref[kk], a_ref[0, kk].reshape(2 * n1, -1), preferred_element_type=F32)
        xr, xi = x[:n1], x[n1:]
        kr, ki = kf_ref[0, kk, 0], kf_ref[0, kk, 1]
        p = jnp.concatenate([xr * kr - xi * ki, xr * ki + xi * kr], axis=0).astype(BF16)
        q = jnp.dot(gt_ref[kk], p, preferred_element_type=F32)
        o_ref[0, kk] = q.reshape(2, n1, -1).astype(o_ref.dtype)


def dft_stage2(g_tab, gt_tab, a5, kf, order):
    B, k2p, _, n1, C = a5.shape
    kb, cb = _tile(TILE_K2, k2p), _tile(TILE_FFT_C, C)
    g_spec = pl.BlockSpec((kb, 2 * n1, 2 * n1), lambda k, c, b: (k, 0, 0))
    return pl.pallas_call(
        _dft_stage2_kernel,
        out_shape=jax.ShapeDtypeStruct(a5.shape, BF16),
        grid=(k2p // kb, C // cb, B),
        in_specs=[g_spec, g_spec,
                  pl.BlockSpec((1, kb, 2, n1, cb), lambda k, c, b: (b, k, 0, 0, c)),
                  pl.BlockSpec((1, kb, 2, n1, cb), lambda k, c, b: (order, k, 0, 0, c))],
        out_specs=pl.BlockSpec((1, kb, 2, n1, cb), lambda k, c, b: (b, k, 0, 0, c)),
        compiler_params=_params("parallel", "arbitrary", "arbitrary"),
        name="dft_stage2",
    )(g_tab, gt_tab, a5, kf)


def _dft_stage3_kernel(w_ref, q_ref, z_ref, gate_ref, skip_ref, o_ref):
    k2p, _, nb, cb = q_ref.shape[1:]
    qt = _swap_major(q_ref[0].astype(F32).reshape(2 * k2p, nb, cb)).astype(BF16)
    w = w_ref[...]
    y = jnp.stack([jnp.dot(w, qt[j], preferred_element_type=F32) for j in range(nb)], axis=0)
    y = _swap_major(y)
    z = z_ref[0].astype(F32)
    o_ref[0] = (gate_ref[0].astype(F32) * (y + z * skip_ref[...])).astype(o_ref.dtype)


def dft_stage3(wi, q5, z4, gate4, skip):
    B, k2p, _, n1, C = q5.shape
    n2h = wi.shape[0]
    nb, cb = BF16_SUBLANES, _tile(TILE_FFT_C, C)
    blk = pl.BlockSpec((1, n2h, nb, cb), lambda b, i, c: (b, 0, i, c))
    return pl.pallas_call(
        _dft_stage3_kernel,
        out_shape=jax.ShapeDtypeStruct((B, n2h, n1, C), BF16),
        grid=(B, n1 // nb, C // cb),
        in_specs=[pl.BlockSpec(wi.shape, lambda b, i, c: (0, 0)),
                  pl.BlockSpec((1, k2p, 2, nb, cb), lambda b, i, c: (b, 0, 0, i, c)),
                  blk, blk, pl.BlockSpec((1, cb), lambda b, i, c: (0, c))],
        out_specs=blk,
        compiler_params=_params("parallel", "arbitrary", "arbitrary"),
        name="dft_stage3",
    )(wi, q5, z4, gate4, skip.reshape(1, C))


def _dft_tables(L):
    n1 = FFT_N1
    n2h = L // n1
    n2 = 2 * n2h
    n = 2 * L
    k2 = n2h + 1
    k2p = -(-k2 // 8) * 8
    kk = np.arange(k2)
    ang1 = 2.0 * np.pi * np.outer(kk, np.arange(n2h)) / n2
    w1 = np.zeros((k2p, 2, n2h))
    w1[:k2, 0], w1[:k2, 1] = np.cos(ang1), -np.sin(ang1)
    c = np.where((kk == 0) | (kk == n2h), 1.0, 2.0)
    wi = np.zeros((n2h, k2p, 2))
    wi[:, :k2, 0], wi[:, :k2, 1] = (c[:, None] * np.cos(ang1)).T / n, (-c[:, None] * np.sin(ang1)).T / n
    freq = kk[:, None, None] + n2 * np.arange(n1)[None, :, None]
    ang2 = 2.0 * np.pi * freq * np.arange(n1)[None, None, :] / n
    gr, gi = np.cos(ang2), -np.sin(ang2)
    g = np.zeros((k2p, 2 * n1, 2 * n1))
    g[:k2] = np.concatenate([np.concatenate([gr, -gi], axis=2), np.concatenate([gi, gr], axis=2)], axis=1)
    to = lambda a: jnp.asarray(a.astype(np.float32)).astype(BF16)
    return to(w1.reshape(2 * k2p, n2h)), to(wi.reshape(n2h, 2 * k2p)), to(g), to(g.transpose(0, 2, 1)), k2p


def _rope_tables(L, hd):
    quarter = hd // 4
    rows_n = L // GRID_W
    row = jnp.repeat(jnp.arange(rows_n, dtype=F32), GRID_W)
    col = jnp.tile(jnp.arange(GRID_W, dtype=F32), rows_n)
    half = hd // 2
    inv = ROPE_THETA ** (-jnp.arange(0, half, 2, dtype=F32) / half)
    ar, ac = row[:, None] * inv[None, :], col[:, None] * inv[None, :]
    cos_t = jnp.concatenate([jnp.cos(ar), jnp.cos(ar), jnp.cos(ac), jnp.cos(ac)], axis=-1)
    sin_t = jnp.concatenate([-jnp.sin(ar), jnp.sin(ar), -jnp.sin(ac), jnp.sin(ac)], axis=-1)
    assert cos_t.shape == (L, 4 * quarter)
    return cos_t, sin_t


def _hyena_pos_features(L, emb):
    bands = (emb - 1) // 2
    t = jnp.linspace(0.0, 1.0, L, dtype=F32)[:, None]
    w = 2.0 * math.pi * jnp.arange(L, dtype=F32)[:, None] / L
    f = jnp.linspace(1e-4, bands - 1, bands, dtype=F32)[None, :]
    ang = w * f
    return jnp.concatenate([t, jnp.cos(ang), -jnp.sin(ang)], axis=-1)


def kernel(x_prompt, x_sample, p_prompt, p_sample, n_ffn1, w_ffn1_up, w_ffn1_down, n_mix, w_in, w_gate, conv_a_w, hy_short_w, hf_w1, hf_b1, hf_freq1, hf_w2, hf_b2, hf_freq2, hf_w3, hf_decay, hf_skip, q_norm, k_norm, w_pa, w_pb, w_pc, w_o, n_ffn2, w_ffn2_up, w_ffn2_down, n_pe, w_pe_gate, w_pe_proj, final_norm):
    assert x_prompt.shape[1:] == x_sample.shape[1:]
    bp, L, D = x_prompt.shape
    B = bp + x_sample.shape[0]
    T = B * L
    depth = w_in.shape[0]
    dc = conv_a_w.shape[-1]
    n_order, dh = hf_skip.shape[1:]
    hd = q_norm.shape[-1]
    qw = w_pc.shape[1]
    kvw = (w_in.shape[-1] - 3 * dc - 3 * dh - qw) // 2
    n_q, n_kv = qw // hd, kvw // hd
    emb = hf_w1.shape[1]
    assert n_order == 2 and hf_w3.shape[-1] == 4 * dh and L % (FFT_N1 * 8) == 0 and L % GRID_W == 0

    h = jnp.concatenate([x_prompt, x_sample], axis=0).reshape(T, D)
    pe_all = jnp.concatenate([p_prompt, p_sample], axis=1).astype(BF16).reshape(depth, T, -1)

    w1_tab, wi_tab, g_tab, gt_tab, k2p = _dft_tables(L)
    cos_t, sin_t = _rope_tables(L, hd)
    emb_p = -(-emb // 8) * 8
    zfeat = jnp.pad(_hyena_pos_features(L, emb), ((0, 0), (0, emb_p - emb)))
    bwd_mask = jnp.tile(jnp.concatenate([jnp.zeros((dh,), F32), jnp.ones((dh,), F32)]), n_order)
    n2h = L // FFT_N1

    def long_conv(z, gate, kf, order, skip):
        z4 = z.reshape(B, n2h, FFT_N1, dh)
        a = dft_stage1(w1_tab, z4)
        q = dft_stage2(g_tab, gt_tab, a, kf, order)
        return dft_stage3(wi_tab, q, z4, gate.reshape(B, n2h, FFT_N1, dh), skip).reshape(B, L, dh)

    for i in range(depth):
        bf = lambda w: w[i].astype(BF16)
        act = norm_swiglu_up(h, n_ffn1[i], bf(w_ffn1_up))
        h = matmul_residual(act, bf(w_ffn1_down), h, 0.5)
        proj = norm_matmul(h, n_mix[i], bf(w_in)).reshape(B, L, -1)
        gates = norm_matmul(h, n_mix[i], bf(w_gate), act="sigmoid")
        ya = conv_a(proj, conv_a_w[i], dc)
        hv, hx1, hx2 = hyena_short_conv(proj, hy_short_w[i], 3 * dc, dh)
        w1p = jnp.pad(hf_w1[i], ((0, emb_p - emb), (0, 0)))
        v0, v1, v2 = _bf16_pieces(hf_w3[i])
        w3s = jnp.concatenate([v0, v1, v0, v2, v0, v1], axis=0).astype(BF16)
        taps, l1 = hyena_filter_taps(zfeat, w1p, hf_b1[i], hf_freq1[i], hf_w2[i], hf_b2[i], hf_freq2[i],
                                     w3s, hf_decay[i], bwd_mask)
        af = dft_stage1(w1_tab, taps.reshape(1, n2h, FFT_N1, 4 * dh))
        kf = filter_spectrum(g_tab, af, l1, n_order, dh)
        zb = long_conv(hv, hx1, kf, 0, hf_skip[i, 0])
        zb = long_conv(zb, hx2, kf, 1, hf_skip[i, 1])
        q_col0 = 3 * dc + 3 * dh
        qr = q_prep(proj, q_norm[i], cos_t, sin_t, q_col0, n_q, hd)
        kt, vx = kv_prep(proj, k_norm[i], cos_t, sin_t, q_col0 + qw, q_col0 + qw + kvw, n_kv, hd,
                         _tile(TILE_K, L))
        yc = flash_attention(qr, kt, vx)
        m = gated_merge(ya.reshape(T, dc), zb.reshape(T, dh), yc.reshape(T, qw), gates,
                        bf(w_pa), bf(w_pb), bf(w_pc))
        h = matmul_residual(m, bf(w_o), h, 1.0)
        act = norm_swiglu_up(h, n_ffn2[i], bf(w_ffn2_up))
        h = matmul_residual(act, bf(w_ffn2_down), h, 0.5)
        h = pe_inject(h, n_pe[i], bf(w_pe_gate), pe_all[i], bf(w_pe_proj))

    y = rms_norm(h, final_norm).reshape(B, L, D)
    return (y[:bp], y[bp:])
```

```python
import functools
import math

import numpy as np
import jax
import jax.numpy as jnp
from jax import lax
from jax.experimental import pallas as pl
from jax.experimental.pallas import tpu as pltpu

F32 = jnp.float32
BF16 = jnp.bfloat16

RMS_EPS = 1e-6
ROPE_THETA = 10000.0
GRID_W = 64
LANES = 128
BF16_SUBLANES = 16
FFT_N1 = 128
VMEM_LIMIT_BYTES = 52 * 1024 * 1024

TILE_M = 1024
TILE_M_DOWN = 1024
TILE_N = 512
TILE_N_WIDE = 1536
TILE_K_SHORT = 2048
TILE_L = 1024
TILE_C = 512
TILE_Q = 512
TILE_K = 1024
TILE_K2 = 8
TILE_FFT_C = 512
TILE_HF = 256


def _tile(default, dim):
    t = min(default, dim)
    assert dim % t == 0, (default, dim)
    return t


def _params(*sem):
    return pltpu.CompilerParams(dimension_semantics=sem, vmem_limit_bytes=VMEM_LIMIT_BYTES)


def _rms_rows(x, g):
    ms = jnp.mean(x * x, axis=-1, keepdims=True)
    return x * lax.rsqrt(ms + RMS_EPS) * g


def _norm_mm_kernel(x_ref, g_ref, w_ref, o_ref, xn_ref, *, act):
    @pl.when(pl.program_id(1) == 0)
    def _():
        xn_ref[...] = _rms_rows(x_ref[...], g_ref[...]).astype(BF16)

    y = jnp.dot(xn_ref[...], w_ref[...], preferred_element_type=F32)
    if act == "sigmoid":
        y = jax.nn.sigmoid(y)
    o_ref[...] = y.astype(o_ref.dtype)


def norm_matmul(x, g, w, act=None):
    T, D = x.shape
    N = w.shape[1]
    tm = _tile(TILE_M, T)
    tn = TILE_N_WIDE if N % TILE_N_WIDE == 0 else _tile(TILE_N, N)
    return pl.pallas_call(
        functools.partial(_norm_mm_kernel, act=act),
        out_shape=jax.ShapeDtypeStruct((T, N), BF16),
        grid=(T // tm, N // tn),
        in_specs=[
            pl.BlockSpec((tm, D), lambda i, j: (i, 0)),
            pl.BlockSpec((1, D), lambda i, j: (0, 0)),
            pl.BlockSpec((D, tn), lambda i, j: (0, j)),
        ],
        out_specs=pl.BlockSpec((tm, tn), lambda i, j: (i, j)),
        scratch_shapes=[pltpu.VMEM((tm, D), BF16)],
        compiler_params=_params("parallel", "arbitrary"),
        name="norm_matmul",
    )(x, g.reshape(1, D), w)


def _norm_swiglu_kernel(x_ref, g_ref, wa_ref, wb_ref, o_ref, xn_ref):
    @pl.when(pl.program_id(1) == 0)
    def _():
        xn_ref[...] = _rms_rows(x_ref[...], g_ref[...]).astype(BF16)

    xn = xn_ref[...]
    a = jnp.dot(xn, wa_ref[...], preferred_element_type=F32)
    b = jnp.dot(xn, wb_ref[...], preferred_element_type=F32)
    o_ref[...] = (a * jax.nn.sigmoid(a) * b).astype(o_ref.dtype)


def norm_swiglu_up(x, g, w_up):
    T, D = x.shape
    F = w_up.shape[1] // 2
    tm, tn = _tile(TILE_M, T), _tile(TILE_N, F)
    nb = F // tn
    return pl.pallas_call(
        _norm_swiglu_kernel,
        out_shape=jax.ShapeDtypeStruct((T, F), BF16),
        grid=(T // tm, nb),
        in_specs=[
            pl.BlockSpec((tm, D), lambda i, j: (i, 0)),
            pl.BlockSpec((1, D), lambda i, j: (0, 0)),
            pl.BlockSpec((D, tn), lambda i, j: (0, j)),
            pl.BlockSpec((D, tn), lambda i, j: (0, nb + j)),
        ],
        out_specs=pl.BlockSpec((tm, tn), lambda i, j: (i, j)),
        scratch_shapes=[pltpu.VMEM((tm, D), BF16)],
        compiler_params=_params("parallel", "arbitrary"),
        name="norm_swiglu_up",
    )(x, g.reshape(1, D), w_up, w_up)


def _mm_residual_kernel(a_ref, w_ref, r_ref, o_ref, *, scale):
    y = jnp.dot(a_ref[...], w_ref[...], preferred_element_type=F32)
    o_ref[...] = r_ref[...] + scale * y


def matmul_residual(a, w, res, scale):
    T, K = a.shape
    N = w.shape[1]
    tm, tn = _tile(TILE_M_DOWN, T), _tile(TILE_N if K > TILE_K_SHORT else 2 * TILE_N, N)
    return pl.pallas_call(
        functools.partial(_mm_residual_kernel, scale=scale),
        out_shape=jax.ShapeDtypeStruct((T, N), F32),
        grid=(T // tm, N // tn),
        in_specs=[
            pl.BlockSpec((tm, K), lambda i, j: (i, 0)),
            pl.BlockSpec((K, tn), lambda i, j: (0, j)),
            pl.BlockSpec((tm, tn), lambda i, j: (i, j)),
        ],
        out_specs=pl.BlockSpec((tm, tn), lambda i, j: (i, j)),
        compiler_params=_params("parallel", "arbitrary"),
        name="matmul_residual",
    )(a, w, res)


def _merge_kernel(ya_ref, zb_ref, yc_ref, ga_ref, gb_ref, gc_ref, wa_ref, wb_ref, wc_ref, o_ref):
    pa = jnp.dot(ya_ref[...], wa_ref[...], preferred_element_type=F32)
    pb = jnp.dot(zb_ref[...], wb_ref[...], preferred_element_type=F32)
    pc = jnp.dot(yc_ref[...], wc_ref[...], preferred_element_type=F32)
    m = ga_ref[...].astype(F32) * pa + gb_ref[...].astype(F32) * pb + gc_ref[...].astype(F32) * pc
    o_ref[...] = m.astype(o_ref.dtype)


def gated_merge(ya, zb, yc, gates, w_pa, w_pb, w_pc):
    T = ya.shape[0]
    D = w_pa.shape[1]
    tm, tn = _tile(TILE_M, T), _tile(TILE_N, D)
    nb = D // tn
    row = lambda a: pl.BlockSpec((tm, a.shape[1]), lambda i, j: (i, 0))
    col = lambda w: pl.BlockSpec((w.shape[0], tn), lambda i, j: (0, j))
    gate = lambda b: pl.BlockSpec((tm, tn), lambda i, j: (i, b * nb + j))
    return pl.pallas_call(
        _merge_kernel,
        out_shape=jax.ShapeDtypeStruct((T, D), BF16),
        grid=(T // tm, nb),
        in_specs=[row(ya), row(zb), row(yc), gate(0), gate(1), gate(2), col(w_pa), col(w_pb), col(w_pc)],
        out_specs=pl.BlockSpec((tm, tn), lambda i, j: (i, j)),
        compiler_params=_params("parallel", "arbitrary"),
        name="gated_merge",
    )(ya, zb, yc, gates, gates, gates, w_pa, w_pb, w_pc)


def _pe_kernel(x_ref, g_ref, wg_ref, pe_ref, wp_ref, r_ref, o_ref, xn_ref):
    @pl.when(pl.program_id(1) == 0)
    def _():
        xn_ref[...] = _rms_rows(x_ref[...], g_ref[...]).astype(BF16)

    gate = jax.nn.sigmoid(jnp.dot(xn_ref[...], wg_ref[...], preferred_element_type=F32))
    emb = jnp.dot(pe_ref[...], wp_ref[...], preferred_element_type=F32)
    o_ref[...] = r_ref[...] + gate * emb


def pe_inject(h, g, w_gate, pe_all, layer, w_proj):
    T, D = h.shape
    P = pe_all.shape[2]
    tm, tn = _tile(TILE_M, T), _tile(TILE_N, D)
    return pl.pallas_call(
        _pe_kernel,
        out_shape=jax.ShapeDtypeStruct((T, D), F32),
        grid=(T // tm, D // tn),
        in_specs=[
            pl.BlockSpec((tm, D), lambda i, j: (i, 0)),
            pl.BlockSpec((1, D), lambda i, j: (0, 0)),
            pl.BlockSpec((D, tn), lambda i, j: (0, j)),
            pl.BlockSpec((None, tm, P), lambda i, j: (layer, i, 0)),
            pl.BlockSpec((P, tn), lambda i, j: (0, j)),
            pl.BlockSpec((tm, tn), lambda i, j: (i, j)),
        ],
        out_specs=pl.BlockSpec((tm, tn), lambda i, j: (i, j)),
        scratch_shapes=[pltpu.VMEM((tm, D), BF16)],
        compiler_params=_params("parallel", "arbitrary"),
        name="pe_inject",
    )(h, g.reshape(1, D), w_gate, pe_all, w_proj, h)


def _rmsnorm_kernel(x_ref, g_ref, o_ref):
    o_ref[...] = _rms_rows(x_ref[...], g_ref[...])


def rms_norm(x, g, row0, rows):
    D = x.shape[1]
    tm = _tile(TILE_M_DOWN, math.gcd(row0, rows) if row0 else rows)
    return pl.pallas_call(
        _rmsnorm_kernel,
        out_shape=jax.ShapeDtypeStruct((rows, D), F32),
        grid=(rows // tm,),
        in_specs=[pl.BlockSpec((tm, D), lambda i: (row0 // tm + i, 0)), pl.BlockSpec((1, D), lambda i: (0, 0))],
        out_specs=pl.BlockSpec((tm, D), lambda i: (i, 0)),
        compiler_params=_params("parallel"),
        name="rms_norm",
    )(x, g.reshape(1, D))


def _conv3_rows(x, prev_row, next_row, w):
    tl = x.shape[0]
    rows = lax.broadcasted_iota(jnp.int32, x.shape, 0)
    xm = jnp.where(rows == 0, prev_row, pltpu.roll(x, 1, 0))
    xp = jnp.where(rows == tl - 1, next_row, pltpu.roll(x, tl - 1, 0))
    return xm * w[0:1] + x * w[1:2] + xp * w[2:3]


def _halo_specs(tl, tc, n_l, col_block):
    hb = tl // BF16_SUBLANES
    n_h = n_l * hb
    main = pl.BlockSpec((1, tl, tc), lambda b, l, c: (b, l, col_block(c)))
    prev = pl.BlockSpec((1, BF16_SUBLANES, tc), lambda b, l, c: (b, jnp.maximum(l * hb - 1, 0), col_block(c)))
    nxt = pl.BlockSpec((1, BF16_SUBLANES, tc), lambda b, l, c: (b, jnp.minimum((l + 1) * hb, n_h - 1), col_block(c)))
    return main, prev, nxt


def _edge_rows(prev_ref, next_ref):
    l = pl.program_id(1)
    prev = prev_ref[0].astype(F32)[BF16_SUBLANES - 1:BF16_SUBLANES]
    nxt = next_ref[0].astype(F32)[0:1]
    prev = jnp.where(l == 0, 0.0, prev)
    nxt = jnp.where(l == pl.num_programs(1) - 1, 0.0, nxt)
    return prev, nxt


def _conv_a_kernel(b_ref, c_ref, cp_ref, cn_ref, x_ref, xp_ref, xn_ref, w_ref, o_ref):
    cprev, cnext = _edge_rows(cp_ref, cn_ref)
    xprev, xnext = _edge_rows(xp_ref, xn_ref)
    p = c_ref[0].astype(F32) * x_ref[0].astype(F32)
    y = _conv3_rows(p, cprev * xprev, cnext * xnext, w_ref[...])
    o_ref[0] = (b_ref[0].astype(F32) * y).astype(o_ref.dtype)


def conv_a(proj, w, dc):
    B, L, _ = proj.shape
    tl, tc = _tile(TILE_L, L), _tile(TILE_C, dc)
    nc, nl = dc // tc, L // tl
    b_spec = pl.BlockSpec((1, tl, tc), lambda b, l, c: (b, l, c))
    c_main, c_prev, c_next = _halo_specs(tl, tc, nl, lambda c: nc + c)
    x_main, x_prev, x_next = _halo_specs(tl, tc, nl, lambda c: 2 * nc + c)
    return pl.pallas_call(
        _conv_a_kernel,
        out_shape=jax.ShapeDtypeStruct((B, L, dc), BF16),
        grid=(B, nl, nc),
        in_specs=[b_spec, c_main, c_prev, c_next, x_main, x_prev, x_next,
                  pl.BlockSpec((3, tc), lambda b, l, c: (0, c))],
        out_specs=pl.BlockSpec((1, tl, tc), lambda b, l, c: (b, l, c)),
        compiler_params=_params("parallel", "arbitrary", "arbitrary"),
        name="conv_a",
    )(proj, proj, proj, proj, proj, proj, proj, w)


def _hy_short_kernel(*refs):
    ins, w_refs, outs = refs[:9], refs[9:12], refs[12:]
    for s in range(3):
        x_ref, p_ref, n_ref = ins[3 * s:3 * s + 3]
        prev, nxt = _edge_rows(p_ref, n_ref)
        y = _conv3_rows(x_ref[0].astype(F32), prev, nxt, w_refs[s][...])
        outs[s][0] = y.astype(outs[s].dtype)


def hyena_short_conv(proj, w, col0, dh):
    B, L, _ = proj.shape
    tl, tc = _tile(TILE_L, L), _tile(TILE_C, dh)
    nc, nl = dh // tc, L // tl
    base = col0 // tc
    in_specs, w_specs = [], []
    for s in range(3):
        in_specs += list(_halo_specs(tl, tc, nl, lambda c, s=s: base + s * nc + c))
        w_specs.append(pl.BlockSpec((3, tc), lambda b, l, c, s=s: (0, s * nc + c)))
    out_spec = pl.BlockSpec((1, tl, tc), lambda b, l, c: (b, l, c))
    out = jax.ShapeDtypeStruct((B, L, dh), BF16)
    return pl.pallas_call(
        _hy_short_kernel,
        out_shape=(out, out, out),
        grid=(B, nl, nc),
        in_specs=in_specs + w_specs,
        out_specs=(out_spec, out_spec, out_spec),
        compiler_params=_params("parallel", "arbitrary", "arbitrary"),
        name="hyena_short_conv",
    )(*([proj] * 9), w, w, w)


def _norm_rope(x, g, cos_t, sin_t):
    xn = _rms_rows(x, g)
    lane = lax.broadcasted_iota(jnp.int32, xn.shape, 1)
    quarter = xn.shape[1] // 4
    swapped = jnp.where((lane % (2 * quarter)) < quarter,
                        pltpu.roll(xn, 3 * quarter, 1), pltpu.roll(xn, quarter, 1))
    return xn * cos_t + swapped * sin_t


def _q_prep_kernel(x_ref, g_ref, cos_ref, sin_ref, o_ref, *, scale):
    hd = o_ref.shape[3]
    for h in range(o_ref.shape[1]):
        y = _norm_rope(x_ref[0, :, h * hd:(h + 1) * hd].astype(F32), g_ref[...], cos_ref[...], sin_ref[...])
        o_ref[0, h] = (y * scale).astype(o_ref.dtype)


def q_prep(proj, gain, cos_t, sin_t, col0, n_q, hd):
    B, L, _ = proj.shape
    tl = _tile(TILE_L, L)
    hb = max(d for d in range(1, n_q + 1) if n_q % d == 0 and col0 % (d * hd) == 0)
    base = col0 // (hb * hd)
    return pl.pallas_call(
        functools.partial(_q_prep_kernel, scale=hd ** -0.5),
        out_shape=jax.ShapeDtypeStruct((B, n_q, L, hd), BF16),
        grid=(B, L // tl, n_q // hb),
        in_specs=[
            pl.BlockSpec((1, tl, hb * hd), lambda b, l, h: (b, l, base + h)),
            pl.BlockSpec((1, hd), lambda b, l, h: (0, 0)),
            pl.BlockSpec((tl, hd), lambda b, l, h: (l, 0)),
            pl.BlockSpec((tl, hd), lambda b, l, h: (l, 0)),
        ],
        out_specs=pl.BlockSpec((1, hb, tl, hd), lambda b, l, h: (b, h, l, 0)),
        compiler_params=_params("parallel", "arbitrary", "arbitrary"),
        name="q_prep",
    )(proj, gain.reshape(1, hd), cos_t, sin_t)


def _kv_prep_kernel(k_ref, v_ref, g_ref, cos_ref, sin_ref, kt_ref, vx_ref):
    y = _norm_rope(k_ref[0].astype(F32), g_ref[...], cos_ref[...], sin_ref[...])
    kt_ref[0, 0, 0] = y.T.astype(kt_ref.dtype)
    v = v_ref[0]
    vx_ref[0, 0] = jnp.concatenate([v, jnp.ones_like(v)], axis=1)


def kv_prep(proj, gain, cos_t, sin_t, k_col0, v_col0, n_kv, hd, tk):
    B, L, _ = proj.shape
    k_base, v_base = k_col0 // hd, v_col0 // hd
    return pl.pallas_call(
        _kv_prep_kernel,
        out_shape=(jax.ShapeDtypeStruct((B, n_kv, L // tk, hd, tk), BF16),
                   jax.ShapeDtypeStruct((B, n_kv, L, 2 * hd), BF16)),
        grid=(B, L // tk, n_kv),
        in_specs=[
            pl.BlockSpec((1, tk, hd), lambda b, l, h: (b, l, k_base + h)),
            pl.BlockSpec((1, tk, hd), lambda b, l, h: (b, l, v_base + h)),
            pl.BlockSpec((1, hd), lambda b, l, h: (0, 0)),
            pl.BlockSpec((tk, hd), lambda b, l, h: (l, 0)),
            pl.BlockSpec((tk, hd), lambda b, l, h: (l, 0)),
        ],
        out_specs=(pl.BlockSpec((1, 1, 1, hd, tk), lambda b, l, h: (b, h, l, 0, 0)),
                   pl.BlockSpec((1, 1, tk, 2 * hd), lambda b, l, h: (b, h, l, 0))),
        compiler_params=_params("parallel", "arbitrary", "arbitrary"),
        name="kv_prep",
    )(proj, proj, gain.reshape(1, hd), cos_t, sin_t)


def _flash_kernel(q_ref, kt_ref, vx_ref, o_ref, s_ref, m_ref, acc_ref):
    group, tq, hd = q_ref.shape[1:]
    nk, _, tk = kt_ref.shape[2:]
    q = q_ref[0].reshape(group * tq, hd)

    def scores(j):
        return jnp.dot(q, kt_ref[0, 0, j], preferred_element_type=F32)

    m_ref[...] = jnp.full(m_ref.shape, -jnp.inf, F32)
    acc_ref[...] = jnp.zeros(acc_ref.shape, F32)
    s_ref[0] = scores(0)

    def step(j, slot):
        s_ref[1 - slot] = scores(jnp.minimum(j + 1, nk - 1))
        s = s_ref[slot]
        m_prev = m_ref[...]
        m_new = jnp.maximum(m_prev, jnp.broadcast_to(jnp.max(s, axis=1, keepdims=True), m_prev.shape))
        alpha = jnp.exp(m_prev - m_new)
        p = jnp.exp(s - jnp.concatenate([m_new] * (tk // LANES), axis=1))
        vj = vx_ref[0, 0, pl.ds(pl.multiple_of(j * tk, tk), tk), :]
        acc_ref[...] = (jnp.concatenate([alpha] * (2 * hd // LANES), axis=1) * acc_ref[...]
                        + jnp.dot(p.astype(BF16), vj, preferred_element_type=F32))
        m_ref[...] = m_new

    def body(i, carry):
        step(2 * i, 0)
        step(2 * i + 1, 1)
        return carry

    lax.fori_loop(0, nk // 2, body, 0)
    acc = acc_ref[...]
    out = acc[:, :hd] / acc[:, hd:]
    for g in range(group):
        o_ref[0, :, g * hd:(g + 1) * hd] = out[g * tq:(g + 1) * tq].astype(o_ref.dtype)


def flash_attention(q, kt, vx):
    B, n_q, L, hd = q.shape
    _, n_kv, nk, _, tk = kt.shape
    assert hd == LANES and nk % 2 == 0
    group = n_q // n_kv
    tq = _tile(TILE_Q, L)
    return pl.pallas_call(
        _flash_kernel,
        out_shape=jax.ShapeDtypeStruct((B, L, n_q * hd), BF16),
        grid=(B, n_kv, L // tq),
        in_specs=[
            pl.BlockSpec((1, group, tq, hd), lambda b, h, i: (b, h, i, 0)),
            pl.BlockSpec((1, 1, nk, hd, tk), lambda b, h, i: (b, h, 0, 0, 0)),
            pl.BlockSpec((1, 1, L, 2 * hd), lambda b, h, i: (b, h, 0, 0)),
        ],
        out_specs=pl.BlockSpec((1, tq, group * hd), lambda b, h, i: (b, i, h)),
        scratch_shapes=[
            pltpu.VMEM((2, group * tq, tk), F32),
            pltpu.VMEM((group * tq, LANES), F32),
            pltpu.VMEM((group * tq, 2 * hd), F32),
        ],
        compiler_params=_params("parallel", "parallel", "arbitrary"),
        name="flash_attention",
    )(q, kt, vx)


def _bf16_pieces(x):
    p0 = x.astype(BF16).astype(F32)
    r = x - p0
    p1 = r.astype(BF16).astype(F32)
    p2 = (r - p1).astype(BF16).astype(F32)
    return p0, p1, p2


def _hyena_filter_kernel(z_ref, w1_ref, b1_ref, f1_ref, w2_ref, b2_ref, f2_ref, w3_ref, dec_ref, bwd_ref,
                         h_ref, l1_ref):
    i = pl.program_id(0)
    hi = lax.Precision.HIGHEST
    z = z_ref[...]
    h = jnp.sin(f1_ref[...] * (jnp.dot(z, w1_ref[...], precision=hi, preferred_element_type=F32) + b1_ref[...]))
    h = jnp.sin(f2_ref[...] * (jnp.dot(h, w2_ref[...], precision=hi, preferred_element_type=F32) + b2_ref[...]))
    h0, h1, h2 = _bf16_pieces(h)
    lhs = jnp.concatenate([h0, h0, h1, h0, h2, h1], axis=1).astype(BF16)
    h = jnp.dot(lhs, w3_ref[...], preferred_element_type=F32)
    h = h * jnp.exp(-z[:, 0:1] * jnp.abs(dec_ref[...]))
    rows = lax.broadcasted_iota(jnp.int32, h.shape, 0)
    h = jnp.where((rows == 0) & (i == 0) & (bwd_ref[...] > 0.5), 0.0, h)

    @pl.when(i == 0)
    def _():
        l1_ref[...] = jnp.zeros(l1_ref.shape, F32)

    l1_ref[...] += jnp.sum(jnp.abs(h), axis=0, keepdims=True)
    h_ref[...] = h.astype(h_ref.dtype)


def hyena_filter_taps(zfeat, w1, b1, f1, w2, b2, f2, w3, decay, bwd_mask):
    L, E = zfeat.shape
    H = w2.shape[0]
    C4 = w3.shape[1]
    tt = _tile(TILE_HF, L)
    full = lambda a: pl.BlockSpec(a.shape, lambda i: (0, 0))
    args = (zfeat, w1, b1.reshape(1, H), f1.reshape(1, H), w2, b2.reshape(1, H), f2.reshape(1, H), w3,
            decay.reshape(1, C4), bwd_mask.reshape(1, C4))
    return pl.pallas_call(
        _hyena_filter_kernel,
        out_shape=(jax.ShapeDtypeStruct((L, C4), BF16), jax.ShapeDtypeStruct((1, C4), F32)),
        grid=(L // tt,),
        in_specs=[pl.BlockSpec((tt, E), lambda i: (i, 0))] + [full(a) for a in args[1:]],
        out_specs=(pl.BlockSpec((tt, C4), lambda i: (i, 0)), pl.BlockSpec((1, C4), lambda i: (0, 0))),
        compiler_params=_params("arbitrary"),
        name="hyena_filter_taps",
    )(*args)


def _swap_major(x):
    return pltpu.einshape("abc->bac", x)


def _dft_stage1_kernel(w_ref, z_ref, o_ref):
    nb = z_ref.shape[2]
    k2p = o_ref.shape[1]
    zt = _swap_major(z_ref[0].astype(F32)).astype(BF16)
    w = w_ref[...]
    r = jnp.stack([jnp.dot(w, zt[j], preferred_element_type=F32) for j in range(nb)], axis=0)
    o_ref[0] = _swap_major(r).reshape(k2p, 2, nb, -1).astype(o_ref.dtype)


def dft_stage1(w1, z4):
    B, n2h, n1, C = z4.shape
    k2p = w1.shape[0] // 2
    nb, cb = BF16_SUBLANES, _tile(TILE_FFT_C, C)
    return pl.pallas_call(
        _dft_stage1_kernel,
        out_shape=jax.ShapeDtypeStruct((B, k2p, 2, n1, C), BF16),
        grid=(B, n1 // nb, C // cb),
        in_specs=[pl.BlockSpec(w1.shape, lambda b, i, c: (0, 0)),
                  pl.BlockSpec((1, n2h, nb, cb), lambda b, i, c: (b, 0, i, c))],
        out_specs=pl.BlockSpec((1, k2p, 2, nb, cb), lambda b, i, c: (b, 0, 0, i, c)),
        compiler_params=_params("parallel", "arbitrary", "arbitrary"),
        name="dft_stage1",
    )(w1, z4)


def _filter_spectrum_kernel(g_ref, af_ref, ab_ref, lf_ref, lb_ref, o_ref):
    n1 = af_ref.shape[3]
    inv_l1 = 1.0 / (lf_ref[...] + lb_ref[...])
    kb = af_ref.shape[1]
    xfs = [jnp.dot(g_ref[kk], af_ref[0, kk].reshape(2 * n1, -1), preferred_element_type=F32) for kk in range(kb)]
    xbs = [jnp.dot(g_ref[kk], ab_ref[0, kk].reshape(2 * n1, -1), preferred_element_type=F32) for kk in range(kb)]
    for kk in range(kb):
        o_ref[0, kk, 0] = (xfs[kk][:n1] + xbs[kk][:n1]) * inv_l1
        o_ref[0, kk, 1] = (xfs[kk][n1:] - xbs[kk][n1:]) * inv_l1


def filter_spectrum(g_tab, a5, l1, n_order, dh):
    _, k2p, _, n1, _ = a5.shape
    kb, cb = _tile(TILE_K2, k2p), _tile(TILE_FFT_C, dh)
    nc = dh // cb
    a_spec = lambda d: pl.BlockSpec((1, kb, 2, n1, cb), lambda k, o, c: (0, k, 0, 0, (2 * o + d) * nc + c))
    l_spec = lambda d: pl.BlockSpec((1, cb), lambda k, o, c: (0, (2 * o + d) * nc + c))
    return pl.pallas_call(
        _filter_spectrum_kernel,
        out_shape=jax.ShapeDtypeStruct((n_order, k2p, 2, n1, dh), F32),
        grid=(k2p // kb, n_order, nc),
        in_specs=[pl.BlockSpec((kb, 2 * n1, 2 * n1), lambda k, o, c: (k, 0, 0)),
                  a_spec(0), a_spec(1), l_spec(0), l_spec(1)],
        out_specs=pl.BlockSpec((1, kb, 2, n1, cb), lambda k, o, c: (o, k, 0, 0, c)),
        compiler_params=_params("parallel", "arbitrary", "arbitrary"),
        name="filter_spectrum",
    )(g_tab, a5, a5, l1, l1)


def _dft_stage2_kernel(g_ref, gt_ref, a_ref, kf_ref, o_ref):
    n1 = a_ref.shape[3]
    kb = a_ref.shape[1]
    xs = [jnp.dot(g_ref[kk], a_ref[0, kk].reshape(2 * n1, -1), preferred_element_type=F32) for kk in range(kb)]
    ps = []
    for kk in range(kb):
        xr, xi = xs[kk][:n1], xs[kk][n1:]
        kr, ki = kf_ref[0, kk, 0], kf_ref[0, kk, 1]
        ps.append(jnp.concatenate([xr * kr - xi * ki, xr * ki + xi * kr], axis=0).astype(BF16))
    for kk in range(kb):
        q = jnp.dot(gt_ref[kk], ps[kk], preferred_element_type=F32)
        o_ref[0, kk] = q.reshape(2, n1, -1).astype(o_ref.dtype)


def dft_stage2(g_tab, gt_tab, a5, kf, order):
    B, k2p, _, n1, C = a5.shape
    kb, cb = _tile(TILE_K2, k2p), _tile(TILE_FFT_C, C)
    g_spec = pl.BlockSpec((kb, 2 * n1, 2 * n1), lambda k, c, b: (k, 0, 0))
    return pl.pallas_call(
        _dft_stage2_kernel,
        out_shape=jax.ShapeDtypeStruct(a5.shape, BF16),
        grid=(k2p // kb, C // cb, B),
        in_specs=[g_spec, g_spec,
                  pl.BlockSpec((1, kb, 2, n1, cb), lambda k, c, b: (b, k, 0, 0, c)),
                  pl.BlockSpec((1, kb, 2, n1, cb), lambda k, c, b: (order, k, 0, 0, c))],
        out_specs=pl.BlockSpec((1, kb, 2, n1, cb), lambda k, c, b: (b, k, 0, 0, c)),
        compiler_params=_params("parallel", "arbitrary", "arbitrary"),
        name="dft_stage2",
    )(g_tab, gt_tab, a5, kf)


def _dft_stage3_kernel(w_ref, q_ref, z_ref, gate_ref, skip_ref, o_ref):
    k2p, _, nb, cb = q_ref.shape[1:]
    qt = _swap_major(q_ref[0].astype(F32).reshape(2 * k2p, nb, cb)).astype(BF16)
    w = w_ref[...]
    y = jnp.stack([jnp.dot(w, qt[j], preferred_element_type=F32) for j in range(nb)], axis=0)
    y = _swap_major(y)
    z = z_ref[0].astype(F32)
    o_ref[0] = (gate_ref[0].astype(F32) * (y + z * skip_ref[...])).astype(o_ref.dtype)


def dft_stage3(wi, q5, z4, gate4, skip):
    B, k2p, _, n1, C = q5.shape
    n2h = wi.shape[0]
    nb, cb = BF16_SUBLANES, _tile(TILE_FFT_C, C)
    blk = pl.BlockSpec((1, n2h, nb, cb), lambda b, i, c: (b, 0, i, c))
    return pl.pallas_call(
        _dft_stage3_kernel,
        out_shape=jax.ShapeDtypeStruct((B, n2h, n1, C), BF16),
        grid=(B, n1 // nb, C // cb),
        in_specs=[pl.BlockSpec(wi.shape, lambda b, i, c: (0, 0)),
                  pl.BlockSpec((1, k2p, 2, nb, cb), lambda b, i, c: (b, 0, 0, i, c)),
                  blk, blk, pl.BlockSpec((1, cb), lambda b, i, c: (0, c))],
        out_specs=blk,
        compiler_params=_params("parallel", "arbitrary", "arbitrary"),
        name="dft_stage3",
    )(wi, q5, z4, gate4, skip.reshape(1, C))


def _dft_tables(L):
    n1 = FFT_N1
    n2h = L // n1
    n2 = 2 * n2h
    n = 2 * L
    k2 = n2h + 1
    k2p = -(-k2 // 8) * 8
    kk = np.arange(k2)
    ang1 = 2.0 * np.pi * np.outer(kk, np.arange(n2h)) / n2
    w1 = np.zeros((k2p, 2, n2h))
    w1[:k2, 0], w1[:k2, 1] = np.cos(ang1), -np.sin(ang1)
    c = np.where((kk == 0) | (kk == n2h), 1.0, 2.0)
    wi = np.zeros((n2h, k2p, 2))
    wi[:, :k2, 0], wi[:, :k2, 1] = (c[:, None] * np.cos(ang1)).T / n, (-c[:, None] * np.sin(ang1)).T / n
    freq = kk[:, None, None] + n2 * np.arange(n1)[None, :, None]
    ang2 = 2.0 * np.pi * freq * np.arange(n1)[None, None, :] / n
    gr, gi = np.cos(ang2), -np.sin(ang2)
    g = np.zeros((k2p, 2 * n1, 2 * n1))
    g[:k2] = np.concatenate([np.concatenate([gr, -gi], axis=2), np.concatenate([gi, gr], axis=2)], axis=1)
    to = lambda a: jnp.asarray(a.astype(np.float32)).astype(BF16)
    return to(w1.reshape(2 * k2p, n2h)), to(wi.reshape(n2h, 2 * k2p)), to(g), to(g.transpose(0, 2, 1)), k2p


def _rope_tables(L, hd):
    quarter = hd // 4
    rows_n = L // GRID_W
    row = jnp.repeat(jnp.arange(rows_n, dtype=F32), GRID_W)
    col = jnp.tile(jnp.arange(GRID_W, dtype=F32), rows_n)
    half = hd // 2
    inv = ROPE_THETA ** (-jnp.arange(0, half, 2, dtype=F32) / half)
    ar, ac = row[:, None] * inv[None, :], col[:, None] * inv[None, :]
    cos_t = jnp.concatenate([jnp.cos(ar), jnp.cos(ar), jnp.cos(ac), jnp.cos(ac)], axis=-1)
    sin_t = jnp.concatenate([-jnp.sin(ar), jnp.sin(ar), -jnp.sin(ac), jnp.sin(ac)], axis=-1)
    assert cos_t.shape == (L, 4 * quarter)
    return cos_t, sin_t


def _hyena_pos_features(L, emb):
    bands = (emb - 1) // 2
    t = jnp.linspace(0.0, 1.0, L, dtype=F32)[:, None]
    w = 2.0 * math.pi * jnp.arange(L, dtype=F32)[:, None] / L
    f = jnp.linspace(1e-4, bands - 1, bands, dtype=F32)[None, :]
    ang = w * f
    return jnp.concatenate([t, jnp.cos(ang), -jnp.sin(ang)], axis=-1)


def kernel(x_prompt, x_sample, p_prompt, p_sample, n_ffn1, w_ffn1_up, w_ffn1_down, n_mix, w_in, w_gate, conv_a_w, hy_short_w, hf_w1, hf_b1, hf_freq1, hf_w2, hf_b2, hf_freq2, hf_w3, hf_decay, hf_skip, q_norm, k_norm, w_pa, w_pb, w_pc, w_o, n_ffn2, w_ffn2_up, w_ffn2_down, n_pe, w_pe_gate, w_pe_proj, final_norm):
    assert x_prompt.shape[1:] == x_sample.shape[1:]
    bp, L, D = x_prompt.shape
    B = bp + x_sample.shape[0]
    T = B * L
    depth = w_in.shape[0]
    dc = conv_a_w.shape[-1]
    n_order, dh = hf_skip.shape[1:]
    hd = q_norm.shape[-1]
    qw = w_pc.shape[1]
    kvw = (w_in.shape[-1] - 3 * dc - 3 * dh - qw) // 2
    n_q, n_kv = qw // hd, kvw // hd
    emb = hf_w1.shape[1]
    assert n_order == 2 and hf_w3.shape[-1] == 4 * dh and L % (FFT_N1 * 8) == 0 and L % GRID_W == 0

    h = jnp.concatenate([x_prompt, x_sample], axis=0).reshape(T, D)
    pe_all = jnp.concatenate([p_prompt, p_sample], axis=1).astype(BF16).reshape(depth, T, -1)

    w1_tab, wi_tab, g_tab, gt_tab, k2p = _dft_tables(L)
    cos_t, sin_t = _rope_tables(L, hd)
    emb_p = -(-emb // 8) * 8
    zfeat = jnp.pad(_hyena_pos_features(L, emb), ((0, 0), (0, emb_p - emb)))
    bwd_mask = jnp.tile(jnp.concatenate([jnp.zeros((dh,), F32), jnp.ones((dh,), F32)]), n_order)
    n2h = L // FFT_N1

    def long_conv(z, gate, kf, order, skip):
        z4 = z.reshape(B, n2h, FFT_N1, dh)
        a = dft_stage1(w1_tab, z4)
        q = dft_stage2(g_tab, gt_tab, a, kf, order)
        return dft_stage3(wi_tab, q, z4, gate.reshape(B, n2h, FFT_N1, dh), skip).reshape(B, L, dh)

    for i in range(depth):
        bf = lambda w: w[i].astype(BF16)
        act = norm_swiglu_up(h, n_ffn1[i], bf(w_ffn1_up))
        h = matmul_residual(act, bf(w_ffn1_down), h, 0.5)
        proj = norm_matmul(h, n_mix[i], bf(w_in)).reshape(B, L, -1)
        gates = norm_matmul(h, n_mix[i], bf(w_gate), act="sigmoid")
        ya = conv_a(proj, conv_a_w[i], dc)
        hv, hx1, hx2 = hyena_short_conv(proj, hy_short_w[i], 3 * dc, dh)
        w1p = jnp.pad(hf_w1[i], ((0, emb_p - emb), (0, 0)))
        v0, v1, v2 = _bf16_pieces(hf_w3[i])
        w3s = jnp.concatenate([v0, v1, v0, v2, v0, v1], axis=0).astype(BF16)
        taps, l1 = hyena_filter_taps(zfeat, w1p, hf_b1[i], hf_freq1[i], hf_w2[i], hf_b2[i], hf_freq2[i],
                                     w3s, hf_decay[i], bwd_mask)
        af = dft_stage1(w1_tab, taps.reshape(1, n2h, FFT_N1, 4 * dh))
        kf = filter_spectrum(g_tab, af, l1, n_order, dh)
        zb = long_conv(hv, hx1, kf, 0, hf_skip[i, 0])
        zb = long_conv(zb, hx2, kf, 1, hf_skip[i, 1])
        q_col0 = 3 * dc + 3 * dh
        qr = q_prep(proj, q_norm[i], cos_t, sin_t, q_col0, n_q, hd)
        kt, vx = kv_prep(proj, k_norm[i], cos_t, sin_t, q_col0 + qw, q_col0 + qw + kvw, n_kv, hd,
                         _tile(TILE_K, L))
        yc = flash_attention(qr, kt, vx)
        m = gated_merge(ya.reshape(T, dc), zb.reshape(T, dh), yc.reshape(T, qw), gates,
                        bf(w_pa), bf(w_pb), bf(w_pc))
        h = matmul_residual(m, bf(w_o), h, 1.0)
        act = norm_swiglu_up(h, n_ffn2[i], bf(w_ffn2_up))
        h = matmul_residual(act, bf(w_ffn2_down), h, 0.5)
        h = pe_inject(h, n_pe[i], bf(w_pe_gate), pe_all, i, bf(w_pe_proj))

    y_prompt = rms_norm(h, final_norm, 0, bp * L).reshape(bp, L, D)
    y_sample = rms_norm(h, final_norm, bp * L, (B - bp) * L).reshape(B - bp, L, D)
    return (y_prompt, y_sample)
```

```python
import functools
import math

import numpy as np
import jax
import jax.numpy as jnp
from jax import lax
from jax.experimental import pallas as pl
from jax.experimental.pallas import tpu as pltpu

F32 = jnp.float32
BF16 = jnp.bfloat16

RMS_EPS = 1e-6
ROPE_THETA = 10000.0
GRID_W = 64
LANES = 128
BF16_SUBLANES = 16
FFT_N1 = 128
VMEM_LIMIT_BYTES = 52 * 1024 * 1024

TILE_M = 1024
TILE_M_DOWN = 1024
TILE_N = 512
TILE_N_WIDE = 1536
TILE_K_SHORT = 2048
TILE_L = 1024
TILE_C = 512
TILE_Q = 512
TILE_K = 1024
TILE_K2 = 8
TILE_FFT_C = 512
TILE_HF = 256


def _tile(default, dim):
    t = min(default, dim)
    assert dim % t == 0, (default, dim)
    return t


def _params(*sem):
    return pltpu.CompilerParams(dimension_semantics=sem, vmem_limit_bytes=VMEM_LIMIT_BYTES)


def _rms_rows(x, g):
    ms = jnp.mean(x * x, axis=-1, keepdims=True)
    return x * lax.rsqrt(ms + RMS_EPS) * g


def _norm_mm_kernel(x_ref, g_ref, w_ref, o_ref, xn_ref, *, act):
    @pl.when(pl.program_id(1) == 0)
    def _():
        xn_ref[...] = _rms_rows(x_ref[...], g_ref[...]).astype(BF16)

    y = jnp.dot(xn_ref[...], w_ref[...], preferred_element_type=F32)
    if act == "sigmoid":
        y = jax.nn.sigmoid(y)
    o_ref[...] = y.astype(o_ref.dtype)


def norm_matmul(x, g, w, act=None):
    T, D = x.shape
    N = w.shape[1]
    tm = _tile(TILE_M, T)
    tn = TILE_N_WIDE if N % TILE_N_WIDE == 0 else _tile(TILE_N, N)
    return pl.pallas_call(
        functools.partial(_norm_mm_kernel, act=act),
        out_shape=jax.ShapeDtypeStruct((T, N), BF16),
        grid=(T // tm, N // tn),
        in_specs=[
            pl.BlockSpec((tm, D), lambda i, j: (i, 0)),
            pl.BlockSpec((1, D), lambda i, j: (0, 0)),
            pl.BlockSpec((D, tn), lambda i, j: (0, j)),
        ],
        out_specs=pl.BlockSpec((tm, tn), lambda i, j: (i, j)),
        scratch_shapes=[pltpu.VMEM((tm, D), BF16)],
        compiler_params=_params("parallel", "arbitrary"),
        name="norm_matmul",
    )(x, g.reshape(1, D), w)


def _norm_swiglu_kernel(x_ref, g_ref, wa_ref, wb_ref, o_ref, xn_ref):
    @pl.when(pl.program_id(1) == 0)
    def _():
        xn_ref[...] = _rms_rows(x_ref[...], g_ref[...]).astype(BF16)

    xn = xn_ref[...]
    a = jnp.dot(xn, wa_ref[...], preferred_element_type=F32)
    b = jnp.dot(xn, wb_ref[...], preferred_element_type=F32)
    o_ref[...] = (a * jax.nn.sigmoid(a) * b).astype(o_ref.dtype)


def norm_swiglu_up(x, g, w_up):
    T, D = x.shape
    F = w_up.shape[1] // 2
    tm, tn = _tile(TILE_M, T), _tile(TILE_N, F)
    nb = F // tn
    return pl.pallas_call(
        _norm_swiglu_kernel,
        out_shape=jax.ShapeDtypeStruct((T, F), BF16),
        grid=(T // tm, nb),
        in_specs=[
            pl.BlockSpec((tm, D), lambda i, j: (i, 0)),
            pl.BlockSpec((1, D), lambda i, j: (0, 0)),
            pl.BlockSpec((D, tn), lambda i, j: (0, j)),
            pl.BlockSpec((D, tn), lambda i, j: (0, nb + j)),
        ],
        out_specs=pl.BlockSpec((tm, tn), lambda i, j: (i, j)),
        scratch_shapes=[pltpu.VMEM((tm, D), BF16)],
        compiler_params=_params("parallel", "arbitrary"),
        name="norm_swiglu_up",
    )(x, g.reshape(1, D), w_up, w_up)


def _mm_residual_kernel(a_ref, w_ref, r_ref, o_ref, *, scale):
    y = jnp.dot(a_ref[...], w_ref[...], preferred_element_type=F32)
    o_ref[...] = r_ref[...] + scale * y


def matmul_residual(a, w, res, scale):
    T, K = a.shape
    N = w.shape[1]
    tm, tn = _tile(TILE_M_DOWN, T), _tile(TILE_N if K > TILE_K_SHORT else 2 * TILE_N, N)
    return pl.pallas_call(
        functools.partial(_mm_residual_kernel, scale=scale),
        out_shape=jax.ShapeDtypeStruct((T, N), F32),
        grid=(T // tm, N // tn),
        in_specs=[
            pl.BlockSpec((tm, K), lambda i, j: (i, 0)),
            pl.BlockSpec((K, tn), lambda i, j: (0, j)),
            pl.BlockSpec((tm, tn), lambda i, j: (i, j)),
        ],
        out_specs=pl.BlockSpec((tm, tn), lambda i, j: (i, j)),
        compiler_params=_params("parallel", "arbitrary"),
        name="matmul_residual",
    )(a, w, res)


def _merge_kernel(ya_ref, zb_ref, yc_ref, ga_ref, gb_ref, gc_ref, wa_ref, wb_ref, wc_ref, o_ref):
    pa = jnp.dot(ya_ref[...], wa_ref[...], preferred_element_type=F32)
    pb = jnp.dot(zb_ref[...], wb_ref[...], preferred_element_type=F32)
    pc = jnp.dot(yc_ref[...], wc_ref[...], preferred_element_type=F32)
    m = ga_ref[...].astype(F32) * pa + gb_ref[...].astype(F32) * pb + gc_ref[...].astype(F32) * pc
    o_ref[...] = m.astype(o_ref.dtype)


def gated_merge(ya, zb, yc, gates, w_pa, w_pb, w_pc):
    T = ya.shape[0]
    D = w_pa.shape[1]
    tm, tn = _tile(TILE_M, T), _tile(TILE_N, D)
    nb = D // tn
    row = lambda a: pl.BlockSpec((tm, a.shape[1]), lambda i, j: (i, 0))
    col = lambda w: pl.BlockSpec((w.shape[0], tn), lambda i, j: (0, j))
    gate = lambda b: pl.BlockSpec((tm, tn), lambda i, j: (i, b * nb + j))
    return pl.pallas_call(
        _merge_kernel,
        out_shape=jax.ShapeDtypeStruct((T, D), BF16),
        grid=(T // tm, nb),
        in_specs=[row(ya), row(zb), row(yc), gate(0), gate(1), gate(2), col(w_pa), col(w_pb), col(w_pc)],
        out_specs=pl.BlockSpec((tm, tn), lambda i, j: (i, j)),
        compiler_params=_params("parallel", "arbitrary"),
        name="gated_merge",
    )(ya, zb, yc, gates, gates, gates, w_pa, w_pb, w_pc)


def _pe_kernel(x_ref, g_ref, wg_ref, pe_ref, wp_ref, r_ref, o_ref, xn_ref):
    @pl.when(pl.program_id(1) == 0)
    def _():
        xn_ref[...] = _rms_rows(x_ref[...], g_ref[...]).astype(BF16)

    gate = jax.nn.sigmoid(jnp.dot(xn_ref[...], wg_ref[...], preferred_element_type=F32))
    emb = jnp.dot(pe_ref[...], wp_ref[...], preferred_element_type=F32)
    o_ref[...] = r_ref[...] + gate * emb


def pe_inject(h, g, w_gate, pe_all, layer, w_proj):
    T, D = h.shape
    P = pe_all.shape[2]
    tm, tn = _tile(TILE_M, T), _tile(TILE_N, D)
    return pl.pallas_call(
        _pe_kernel,
        out_shape=jax.ShapeDtypeStruct((T, D), F32),
        grid=(T // tm, D // tn),
        in_specs=[
            pl.BlockSpec((tm, D), lambda i, j: (i, 0)),
            pl.BlockSpec((1, D), lambda i, j: (0, 0)),
            pl.BlockSpec((D, tn), lambda i, j: (0, j)),
            pl.BlockSpec((None, tm, P), lambda i, j: (layer, i, 0)),
            pl.BlockSpec((P, tn), lambda i, j: (0, j)),
            pl.BlockSpec((tm, tn), lambda i, j: (i, j)),
        ],
        out_specs=pl.BlockSpec((tm, tn), lambda i, j: (i, j)),
        scratch_shapes=[pltpu.VMEM((tm, D), BF16)],
        compiler_params=_params("parallel", "arbitrary"),
        name="pe_inject",
    )(h, g.reshape(1, D), w_gate, pe_all, w_proj, h)


def _rmsnorm_kernel(x_ref, g_ref, o_ref):
    o_ref[...] = _rms_rows(x_ref[...], g_ref[...])


def rms_norm(x, g, row0, rows):
    D = x.shape[1]
    tm = _tile(TILE_M_DOWN, math.gcd(row0, rows) if row0 else rows)
    return pl.pallas_call(
        _rmsnorm_kernel,
        out_shape=jax.ShapeDtypeStruct((rows, D), F32),
        grid=(rows // tm,),
        in_specs=[pl.BlockSpec((tm, D), lambda i: (row0 // tm + i, 0)), pl.BlockSpec((1, D), lambda i: (0, 0))],
        out_specs=pl.BlockSpec((tm, D), lambda i: (i, 0)),
        compiler_params=_params("parallel"),
        name="rms_norm",
    )(x, g.reshape(1, D))


def _conv3_rows(x, prev_row, next_row, w):
    tl = x.shape[0]
    rows = lax.broadcasted_iota(jnp.int32, x.shape, 0)
    xm = jnp.where(rows == 0, prev_row, pltpu.roll(x, 1, 0))
    xp = jnp.where(rows == tl - 1, next_row, pltpu.roll(x, tl - 1, 0))
    return xm * w[0:1] + x * w[1:2] + xp * w[2:3]


def _halo_specs(tl, tc, n_l, col_block):
    hb = tl // BF16_SUBLANES
    n_h = n_l * hb
    main = pl.BlockSpec((1, tl, tc), lambda b, l, c: (b, l, col_block(c)))
    prev = pl.BlockSpec((1, BF16_SUBLANES, tc), lambda b, l, c: (b, jnp.maximum(l * hb - 1, 0), col_block(c)))
    nxt = pl.BlockSpec((1, BF16_SUBLANES, tc), lambda b, l, c: (b, jnp.minimum((l + 1) * hb, n_h - 1), col_block(c)))
    return main, prev, nxt


def _edge_rows(prev_ref, next_ref):
    l = pl.program_id(1)
    prev = prev_ref[0].astype(F32)[BF16_SUBLANES - 1:BF16_SUBLANES]
    nxt = next_ref[0].astype(F32)[0:1]
    prev = jnp.where(l == 0, 0.0, prev)
    nxt = jnp.where(l == pl.num_programs(1) - 1, 0.0, nxt)
    return prev, nxt


def _conv_a_kernel(b_ref, c_ref, cp_ref, cn_ref, x_ref, xp_ref, xn_ref, w_ref, o_ref):
    cprev, cnext = _edge_rows(cp_ref, cn_ref)
    xprev, xnext = _edge_rows(xp_ref, xn_ref)
    p = c_ref[0].astype(F32) * x_ref[0].astype(F32)
    y = _conv3_rows(p, cprev * xprev, cnext * xnext, w_ref[...])
    o_ref[0] = (b_ref[0].astype(F32) * y).astype(o_ref.dtype)


def conv_a(proj, w, dc):
    B, L, _ = proj.shape
    tl, tc = _tile(TILE_L, L), _tile(TILE_C, dc)
    nc, nl = dc // tc, L // tl
    b_spec = pl.BlockSpec((1, tl, tc), lambda b, l, c: (b, l, c))
    c_main, c_prev, c_next = _halo_specs(tl, tc, nl, lambda c: nc + c)
    x_main, x_prev, x_next = _halo_specs(tl, tc, nl, lambda c: 2 * nc + c)
    return pl.pallas_call(
        _conv_a_kernel,
        out_shape=jax.ShapeDtypeStruct((B, L, dc), BF16),
        grid=(B, nl, nc),
        in_specs=[b_spec, c_main, c_prev, c_next, x_main, x_prev, x_next,
                  pl.BlockSpec((3, tc), lambda b, l, c: (0, c))],
        out_specs=pl.BlockSpec((1, tl, tc), lambda b, l, c: (b, l, c)),
        compiler_params=_params("parallel", "arbitrary", "arbitrary"),
        name="conv_a",
    )(proj, proj, proj, proj, proj, proj, proj, w)


def _hy_short_kernel(*refs):
    ins, w_refs, outs = refs[:9], refs[9:12], refs[12:]
    for s in range(3):
        x_ref, p_ref, n_ref = ins[3 * s:3 * s + 3]
        prev, nxt = _edge_rows(p_ref, n_ref)
        y = _conv3_rows(x_ref[0].astype(F32), prev, nxt, w_refs[s][...])
        outs[s][0] = y.astype(outs[s].dtype)


def hyena_short_conv(proj, w, col0, dh):
    B, L, _ = proj.shape
    tl, tc = _tile(TILE_L, L), _tile(TILE_C, dh)
    nc, nl = dh // tc, L // tl
    base = col0 // tc
    in_specs, w_specs = [], []
    for s in range(3):
        in_specs += list(_halo_specs(tl, tc, nl, lambda c, s=s: base + s * nc + c))
        w_specs.append(pl.BlockSpec((3, tc), lambda b, l, c, s=s: (0, s * nc + c)))
    out_spec = pl.BlockSpec((1, tl, tc), lambda b, l, c: (b, l, c))
    out = jax.ShapeDtypeStruct((B, L, dh), BF16)
    return pl.pallas_call(
        _hy_short_kernel,
        out_shape=(out, out, out),
        grid=(B, nl, nc),
        in_specs=in_specs + w_specs,
        out_specs=(out_spec, out_spec, out_spec),
        compiler_params=_params("parallel", "arbitrary", "arbitrary"),
        name="hyena_short_conv",
    )(*([proj] * 9), w, w, w)


def _norm_rope(x, g, cos_t, sin_t):
    xn = _rms_rows(x, g)
    lane = lax.broadcasted_iota(jnp.int32, xn.shape, 1)
    quarter = xn.shape[1] // 4
    swapped = jnp.where((lane % (2 * quarter)) < quarter,
                        pltpu.roll(xn, 3 * quarter, 1), pltpu.roll(xn, quarter, 1))
    return xn * cos_t + swapped * sin_t


def _q_prep_kernel(x_ref, g_ref, cos_ref, sin_ref, o_ref, *, scale):
    hd = o_ref.shape[3]
    for h in range(o_ref.shape[1]):
        y = _norm_rope(x_ref[0, :, h * hd:(h + 1) * hd].astype(F32), g_ref[...], cos_ref[...], sin_ref[...])
        o_ref[0, h] = (y * scale).astype(o_ref.dtype)


def q_prep(proj, gain, cos_t, sin_t, col0, n_q, hd):
    B, L, _ = proj.shape
    tl = _tile(TILE_L, L)
    hb = max(d for d in range(1, n_q + 1) if n_q % d == 0 and col0 % (d * hd) == 0)
    base = col0 // (hb * hd)
    return pl.pallas_call(
        functools.partial(_q_prep_kernel, scale=hd ** -0.5),
        out_shape=jax.ShapeDtypeStruct((B, n_q, L, hd), BF16),
        grid=(B, L // tl, n_q // hb),
        in_specs=[
            pl.BlockSpec((1, tl, hb * hd), lambda b, l, h: (b, l, base + h)),
            pl.BlockSpec((1, hd), lambda b, l, h: (0, 0)),
            pl.BlockSpec((tl, hd), lambda b, l, h: (l, 0)),
            pl.BlockSpec((tl, hd), lambda b, l, h: (l, 0)),
        ],
        out_specs=pl.BlockSpec((1, hb, tl, hd), lambda b, l, h: (b, h, l, 0)),
        compiler_params=_params("parallel", "arbitrary", "arbitrary"),
        name="q_prep",
    )(proj, gain.reshape(1, hd), cos_t, sin_t)


def _kv_prep_kernel(k_ref, v_ref, g_ref, cos_ref, sin_ref, kt_ref, vx_ref):
    y = _norm_rope(k_ref[0].astype(F32), g_ref[...], cos_ref[...], sin_ref[...])
    kt_ref[0, 0, 0] = y.T.astype(kt_ref.dtype)
    v = v_ref[0]
    vx_ref[0, 0] = jnp.concatenate([v, jnp.ones_like(v)], axis=1)


def kv_prep(proj, gain, cos_t, sin_t, k_col0, v_col0, n_kv, hd, tk):
    B, L, _ = proj.shape
    k_base, v_base = k_col0 // hd, v_col0 // hd
    return pl.pallas_call(
        _kv_prep_kernel,
        out_shape=(jax.ShapeDtypeStruct((B, n_kv, L // tk, hd, tk), BF16),
                   jax.ShapeDtypeStruct((B, n_kv, L, 2 * hd), BF16)),
        grid=(B, L // tk, n_kv),
        in_specs=[
            pl.BlockSpec((1, tk, hd), lambda b, l, h: (b, l, k_base + h)),
            pl.BlockSpec((1, tk, hd), lambda b, l, h: (b, l, v_base + h)),
            pl.BlockSpec((1, hd), lambda b, l, h: (0, 0)),
            pl.BlockSpec((tk, hd), lambda b, l, h: (l, 0)),
            pl.BlockSpec((tk, hd), lambda b, l, h: (l, 0)),
        ],
        out_specs=(pl.BlockSpec((1, 1, 1, hd, tk), lambda b, l, h: (b, h, l, 0, 0)),
                   pl.BlockSpec((1, 1, tk, 2 * hd), lambda b, l, h: (b, h, l, 0))),
        compiler_params=_params("parallel", "arbitrary", "arbitrary"),
        name="kv_prep",
    )(proj, proj, gain.reshape(1, hd), cos_t, sin_t)


def _flash_kernel(q_ref, kt_ref, vx_ref, o_ref, s_ref, m_ref, acc_ref):
    group, tq, hd = q_ref.shape[1:]
    nk, _, tk = kt_ref.shape[2:]
    q = q_ref[0].reshape(group * tq, hd)

    def scores(j):
        return jnp.dot(q, kt_ref[0, 0, j], preferred_element_type=F32)

    m_ref[...] = jnp.full(m_ref.shape, -jnp.inf, F32)
    acc_ref[...] = jnp.zeros(acc_ref.shape, F32)
    s_ref[0] = scores(0)

    def step(j, slot):
        s_ref[1 - slot] = scores(jnp.minimum(j + 1, nk - 1))
        s = s_ref[slot]
        m_prev = m_ref[...]
        m_new = jnp.maximum(m_prev, jnp.broadcast_to(jnp.max(s, axis=1, keepdims=True), m_prev.shape))
        alpha = jnp.exp(m_prev - m_new)
        p = jnp.exp(s - jnp.concatenate([m_new] * (tk // LANES), axis=1))
        vj = vx_ref[0, 0, pl.ds(pl.multiple_of(j * tk, tk), tk), :]
        acc_ref[...] = (jnp.concatenate([alpha] * (2 * hd // LANES), axis=1) * acc_ref[...]
                        + jnp.dot(p.astype(BF16), vj, preferred_element_type=F32))
        m_ref[...] = m_new

    def body(i, carry):
        step(2 * i, 0)
        step(2 * i + 1, 1)
        return carry

    lax.fori_loop(0, nk // 2, body, 0)
    acc = acc_ref[...]
    out = acc[:, :hd] / acc[:, hd:]
    for g in range(group):
        o_ref[0, :, g * hd:(g + 1) * hd] = out[g * tq:(g + 1) * tq].astype(o_ref.dtype)


def flash_attention(q, kt, vx):
    B, n_q, L, hd = q.shape
    _, n_kv, nk, _, tk = kt.shape
    assert hd == LANES and nk % 2 == 0
    group = n_q // n_kv
    tq = _tile(TILE_Q, L)
    return pl.pallas_call(
        _flash_kernel,
        out_shape=jax.ShapeDtypeStruct((B, L, n_q * hd), BF16),
        grid=(B, n_kv, L // tq),
        in_specs=[
            pl.BlockSpec((1, group, tq, hd), lambda b, h, i: (b, h, i, 0)),
            pl.BlockSpec((1, 1, nk, hd, tk), lambda b, h, i: (b, h, 0, 0, 0)),
            pl.BlockSpec((1, 1, L, 2 * hd), lambda b, h, i: (b, h, 0, 0)),
        ],
        out_specs=pl.BlockSpec((1, tq, group * hd), lambda b, h, i: (b, i, h)),
        scratch_shapes=[
            pltpu.VMEM((2, group * tq, tk), F32),
            pltpu.VMEM((group * tq, LANES), F32),
            pltpu.VMEM((group * tq, 2 * hd), F32),
        ],
        compiler_params=_params("parallel", "parallel", "arbitrary"),
        name="flash_attention",
    )(q, kt, vx)


def _bf16_pieces(x):
    p0 = x.astype(BF16).astype(F32)
    r = x - p0
    p1 = r.astype(BF16).astype(F32)
    p2 = (r - p1).astype(BF16).astype(F32)
    return p0, p1, p2


def _hyena_filter_kernel(z_ref, w1_ref, b1_ref, f1_ref, w2_ref, b2_ref, f2_ref, w3_ref, dec_ref, bwd_ref,
                         h_ref, l1_ref):
    i = pl.program_id(0)
    hi = lax.Precision.HIGHEST
    z = z_ref[...]
    h = jnp.sin(f1_ref[...] * (jnp.dot(z, w1_ref[...], precision=hi, preferred_element_type=F32) + b1_ref[...]))
    h = jnp.sin(f2_ref[...] * (jnp.dot(h, w2_ref[...], precision=hi, preferred_element_type=F32) + b2_ref[...]))
    h0, h1, h2 = _bf16_pieces(h)
    lhs = jnp.concatenate([h0, h0, h1, h0, h2, h1], axis=1).astype(BF16)
    h = jnp.dot(lhs, w3_ref[...], preferred_element_type=F32)
    h = h * jnp.exp(-z[:, 0:1] * jnp.abs(dec_ref[...]))
    rows = lax.broadcasted_iota(jnp.int32, h.shape, 0)
    h = jnp.where((rows == 0) & (i == 0) & (bwd_ref[...] > 0.5), 0.0, h)

    @pl.when(i == 0)
    def _():
        l1_ref[...] = jnp.zeros(l1_ref.shape, F32)

    l1_ref[...] += jnp.sum(jnp.abs(h), axis=0, keepdims=True)
    h_ref[...] = h.astype(h_ref.dtype)


def hyena_filter_taps(zfeat, w1, b1, f1, w2, b2, f2, w3, decay, bwd_mask):
    L, E = zfeat.shape
    H = w2.shape[0]
    C4 = w3.shape[1]
    tt = _tile(TILE_HF, L)
    full = lambda a: pl.BlockSpec(a.shape, lambda i: (0, 0))
    args = (zfeat, w1, b1.reshape(1, H), f1.reshape(1, H), w2, b2.reshape(1, H), f2.reshape(1, H), w3,
            decay.reshape(1, C4), bwd_mask.reshape(1, C4))
    return pl.pallas_call(
        _hyena_filter_kernel,
        out_shape=(jax.ShapeDtypeStruct((L, C4), BF16), jax.ShapeDtypeStruct((1, C4), F32)),
        grid=(L // tt,),
        in_specs=[pl.BlockSpec((tt, E), lambda i: (i, 0))] + [full(a) for a in args[1:]],
        out_specs=(pl.BlockSpec((tt, C4), lambda i: (i, 0)), pl.BlockSpec((1, C4), lambda i: (0, 0))),
        compiler_params=_params("arbitrary"),
        name="hyena_filter_taps",
    )(*args)


def _swap_major(x):
    return pltpu.einshape("abc->bac", x)


def _dft_stage1_kernel(w_ref, z_ref, o_ref):
    nb = z_ref.shape[2]
    k2p = o_ref.shape[1]
    zt = _swap_major(z_ref[0])
    w = w_ref[...]
    r = jnp.stack([jnp.dot(w, zt[j], preferred_element_type=F32) for j in range(nb)], axis=0)
    o_ref[0] = _swap_major(r.astype(o_ref.dtype)).reshape(k2p, 2, nb, -1)


def dft_stage1(w1, z4):
    B, n2h, n1, C = z4.shape
    k2p = w1.shape[0] // 2
    nb, cb = BF16_SUBLANES, _tile(TILE_FFT_C, C)
    return pl.pallas_call(
        _dft_stage1_kernel,
        out_shape=jax.ShapeDtypeStruct((B, k2p, 2, n1, C), BF16),
        grid=(B, n1 // nb, C // cb),
        in_specs=[pl.BlockSpec(w1.shape, lambda b, i, c: (0, 0)),
                  pl.BlockSpec((1, n2h, nb, cb), lambda b, i, c: (b, 0, i, c))],
        out_specs=pl.BlockSpec((1, k2p, 2, nb, cb), lambda b, i, c: (b, 0, 0, i, c)),
        compiler_params=_params("parallel", "arbitrary", "arbitrary"),
        name="dft_stage1",
    )(w1, z4)


def _filter_spectrum_kernel(g_ref, af_ref, ab_ref, lf_ref, lb_ref, o_ref):
    n1 = af_ref.shape[3]
    inv_l1 = 1.0 / (lf_ref[...] + lb_ref[...])
    kb = af_ref.shape[1]
    xfs = [jnp.dot(g_ref[kk], af_ref[0, kk].reshape(2 * n1, -1), preferred_element_type=F32) for kk in range(kb)]
    xbs = [jnp.dot(g_ref[kk], ab_ref[0, kk].reshape(2 * n1, -1), preferred_element_type=F32) for kk in range(kb)]
    for kk in range(kb):
        o_ref[0, kk, 0] = (xfs[kk][:n1] + xbs[kk][:n1]) * inv_l1
        o_ref[0, kk, 1] = (xfs[kk][n1:] - xbs[kk][n1:]) * inv_l1


def filter_spectrum(g_tab, a5, l1, n_order, dh):
    _, k2p, _, n1, _ = a5.shape
    kb, cb = _tile(TILE_K2, k2p), _tile(TILE_FFT_C, dh)
    nc = dh // cb
    a_spec = lambda d: pl.BlockSpec((1, kb, 2, n1, cb), lambda k, o, c: (0, k, 0, 0, (2 * o + d) * nc + c))
    l_spec = lambda d: pl.BlockSpec((1, cb), lambda k, o, c: (0, (2 * o + d) * nc + c))
    return pl.pallas_call(
        _filter_spectrum_kernel,
        out_shape=jax.ShapeDtypeStruct((n_order, k2p, 2, n1, dh), F32),
        grid=(k2p // kb, n_order, nc),
        in_specs=[pl.BlockSpec((kb, 2 * n1, 2 * n1), lambda k, o, c: (k, 0, 0)),
                  a_spec(0), a_spec(1), l_spec(0), l_spec(1)],
        out_specs=pl.BlockSpec((1, kb, 2, n1, cb), lambda k, o, c: (o, k, 0, 0, c)),
        compiler_params=_params("parallel", "arbitrary", "arbitrary"),
        name="filter_spectrum",
    )(g_tab, a5, a5, l1, l1)


def _dft_stage2_kernel(g_ref, gt_ref, a_ref, kf_ref, o_ref):
    n1 = a_ref.shape[3]
    kb = a_ref.shape[1]
    xs = [jnp.dot(g_ref[kk], a_ref[0, kk].reshape(2 * n1, -1), preferred_element_type=F32) for kk in range(kb)]
    ps = []
    for kk in range(kb):
        xr, xi = xs[kk][:n1], xs[kk][n1:]
        kr, ki = kf_ref[0, kk, 0], kf_ref[0, kk, 1]
        ps.append(jnp.concatenate([xr * kr - xi * ki, xr * ki + xi * kr], axis=0).astype(BF16))
    for kk in range(kb):
        q = jnp.dot(gt_ref[kk], ps[kk], preferred_element_type=F32)
        o_ref[0, kk] = q.reshape(2, n1, -1).astype(o_ref.dtype)


def dft_stage2(g_tab, gt_tab, a5, kf, order):
    B, k2p, _, n1, C = a5.shape
    kb, cb = _tile(TILE_K2, k2p), _tile(TILE_FFT_C, C)
    g_spec = pl.BlockSpec((kb, 2 * n1, 2 * n1), lambda k, c, b: (k, 0, 0))
    return pl.pallas_call(
        _dft_stage2_kernel,
        out_shape=jax.ShapeDtypeStruct(a5.shape, BF16),
        grid=(k2p // kb, C // cb, B),
        in_specs=[g_spec, g_spec,
                  pl.BlockSpec((1, kb, 2, n1, cb), lambda k, c, b: (b, k, 0, 0, c)),
                  pl.BlockSpec((1, kb, 2, n1, cb), lambda k, c, b: (order, k, 0, 0, c))],
        out_specs=pl.BlockSpec((1, kb, 2, n1, cb), lambda k, c, b: (b, k, 0, 0, c)),
        compiler_params=_params("parallel", "arbitrary", "arbitrary"),
        name="dft_stage2",
    )(g_tab, gt_tab, a5, kf)


def _dft_stage3_kernel(w_ref, q_ref, z_ref, gate_ref, skip_ref, o_ref):
    k2p, _, nb, cb = q_ref.shape[1:]
    qt = _swap_major(q_ref[0].reshape(2 * k2p, nb, cb))
    w = w_ref[...]
    y = jnp.stack([jnp.dot(w, qt[j], preferred_element_type=F32) for j in range(nb)], axis=0)
    y = _swap_major(y)
    z = z_ref[0].astype(F32)
    o_ref[0] = (gate_ref[0].astype(F32) * (y + z * skip_ref[...])).astype(o_ref.dtype)


def dft_stage3(wi, q5, z4, gate4, skip):
    B, k2p, _, n1, C = q5.shape
    n2h = wi.shape[0]
    nb, cb = BF16_SUBLANES, _tile(TILE_FFT_C, C)
    blk = pl.BlockSpec((1, n2h, nb, cb), lambda b, i, c: (b, 0, i, c))
    return pl.pallas_call(
        _dft_stage3_kernel,
        out_shape=jax.ShapeDtypeStruct((B, n2h, n1, C), BF16),
        grid=(B, n1 // nb, C // cb),
        in_specs=[pl.BlockSpec(wi.shape, lambda b, i, c: (0, 0)),
                  pl.BlockSpec((1, k2p, 2, nb, cb), lambda b, i, c: (b, 0, 0, i, c)),
                  blk, blk, pl.BlockSpec((1, cb), lambda b, i, c: (0, c))],
        out_specs=blk,
        compiler_params=_params("parallel", "arbitrary", "arbitrary"),
        name="dft_stage3",
    )(wi, q5, z4, gate4, skip.reshape(1, C))


def _dft_tables(L):
    n1 = FFT_N1
    n2h = L // n1
    n2 = 2 * n2h
    n = 2 * L
    k2 = n2h + 1
    k2p = -(-k2 // 8) * 8
    kk = np.arange(k2)
    ang1 = 2.0 * np.pi * np.outer(kk, np.arange(n2h)) / n2
    w1 = np.zeros((k2p, 2, n2h))
    w1[:k2, 0], w1[:k2, 1] = np.cos(ang1), -np.sin(ang1)
    c = np.where((kk == 0) | (kk == n2h), 1.0, 2.0)
    wi = np.zeros((n2h, k2p, 2))
    wi[:, :k2, 0], wi[:, :k2, 1] = (c[:, None] * np.cos(ang1)).T / n, (-c[:, None] * np.sin(ang1)).T / n
    freq = kk[:, None, None] + n2 * np.arange(n1)[None, :, None]
    ang2 = 2.0 * np.pi * freq * np.arange(n1)[None, None, :] / n
    gr, gi = np.cos(ang2), -np.sin(ang2)
    g = np.zeros((k2p, 2 * n1, 2 * n1))
    g[:k2] = np.concatenate([np.concatenate([gr, -gi], axis=2), np.concatenate([gi, gr], axis=2)], axis=1)
    to = lambda a: jnp.asarray(a.astype(np.float32)).astype(BF16)
    return to(w1.reshape(2 * k2p, n2h)), to(wi.reshape(n2h, 2 * k2p)), to(g), to(g.transpose(0, 2, 1)), k2p


def _rope_tables(L, hd):
    quarter = hd // 4
    rows_n = L // GRID_W
    row = jnp.repeat(jnp.arange(rows_n, dtype=F32), GRID_W)
    col = jnp.tile(jnp.arange(GRID_W, dtype=F32), rows_n)
    half = hd // 2
    inv = ROPE_THETA ** (-jnp.arange(0, half, 2, dtype=F32) / half)
    ar, ac = row[:, None] * inv[None, :], col[:, None] * inv[None, :]
    cos_t = jnp.concatenate([jnp.cos(ar), jnp.cos(ar), jnp.cos(ac), jnp.cos(ac)], axis=-1)
    sin_t = jnp.concatenate([-jnp.sin(ar), jnp.sin(ar), -jnp.sin(ac), jnp.sin(ac)], axis=-1)
    assert cos_t.shape == (L, 4 * quarter)
    return cos_t, sin_t


def _hyena_pos_features(L, emb):
    bands = (emb - 1) // 2
    t = jnp.linspace(0.0, 1.0, L, dtype=F32)[:, None]
    w = 2.0 * math.pi * jnp.arange(L, dtype=F32)[:, None] / L
    f = jnp.linspace(1e-4, bands - 1, bands, dtype=F32)[None, :]
    ang = w * f
    return jnp.concatenate([t, jnp.cos(ang), -jnp.sin(ang)], axis=-1)


def kernel(x_prompt, x_sample, p_prompt, p_sample, n_ffn1, w_ffn1_up, w_ffn1_down, n_mix, w_in, w_gate, conv_a_w, hy_short_w, hf_w1, hf_b1, hf_freq1, hf_w2, hf_b2, hf_freq2, hf_w3, hf_decay, hf_skip, q_norm, k_norm, w_pa, w_pb, w_pc, w_o, n_ffn2, w_ffn2_up, w_ffn2_down, n_pe, w_pe_gate, w_pe_proj, final_norm):
    assert x_prompt.shape[1:] == x_sample.shape[1:]
    bp, L, D = x_prompt.shape
    B = bp + x_sample.shape[0]
    T = B * L
    depth = w_in.shape[0]
    dc = conv_a_w.shape[-1]
    n_order, dh = hf_skip.shape[1:]
    hd = q_norm.shape[-1]
    qw = w_pc.shape[1]
    kvw = (w_in.shape[-1] - 3 * dc - 3 * dh - qw) // 2
    n_q, n_kv = qw // hd, kvw // hd
    emb = hf_w1.shape[1]
    assert n_order == 2 and hf_w3.shape[-1] == 4 * dh and L % (FFT_N1 * 8) == 0 and L % GRID_W == 0

    h = jnp.concatenate([x_prompt, x_sample], axis=0).reshape(T, D)
    pe_all = jnp.concatenate([p_prompt, p_sample], axis=1).astype(BF16).reshape(depth, T, -1)

    w1_tab, wi_tab, g_tab, gt_tab, k2p = _dft_tables(L)
    cos_t, sin_t = _rope_tables(L, hd)
    emb_p = -(-emb // 8) * 8
    zfeat = jnp.pad(_hyena_pos_features(L, emb), ((0, 0), (0, emb_p - emb)))
    bwd_mask = jnp.tile(jnp.concatenate([jnp.zeros((dh,), F32), jnp.ones((dh,), F32)]), n_order)
    n2h = L // FFT_N1

    def long_conv(z, gate, kf, order, skip):
        z4 = z.reshape(B, n2h, FFT_N1, dh)
        a = dft_stage1(w1_tab, z4)
        q = dft_stage2(g_tab, gt_tab, a, kf, order)
        return dft_stage3(wi_tab, q, z4, gate.reshape(B, n2h, FFT_N1, dh), skip).reshape(B, L, dh)

    for i in range(depth):
        bf = lambda w: w[i].astype(BF16)
        act = norm_swiglu_up(h, n_ffn1[i], bf(w_ffn1_up))
        h = matmul_residual(act, bf(w_ffn1_down), h, 0.5)
        proj = norm_matmul(h, n_mix[i], bf(w_in)).reshape(B, L, -1)
        gates = norm_matmul(h, n_mix[i], bf(w_gate), act="sigmoid")
        ya = conv_a(proj, conv_a_w[i], dc)
        hv, hx1, hx2 = hyena_short_conv(proj, hy_short_w[i], 3 * dc, dh)
        w1p = jnp.pad(hf_w1[i], ((0, emb_p - emb), (0, 0)))
        v0, v1, v2 = _bf16_pieces(hf_w3[i])
        w3s = jnp.concatenate([v0, v1, v0, v2, v0, v1], axis=0).astype(BF16)
        taps, l1 = hyena_filter_taps(zfeat, w1p, hf_b1[i], hf_freq1[i], hf_w2[i], hf_b2[i], hf_freq2[i],
                                     w3s, hf_decay[i], bwd_mask)
        af = dft_stage1(w1_tab, taps.reshape(1, n2h, FFT_N1, 4 * dh))
        kf = filter_spectrum(g_tab, af, l1, n_order, dh)
        zb = long_conv(hv, hx1, kf, 0, hf_skip[i, 0])
        zb = long_conv(zb, hx2, kf, 1, hf_skip[i, 1])
        q_col0 = 3 * dc + 3 * dh
        qr = q_prep(proj, q_norm[i], cos_t, sin_t, q_col0, n_q, hd)
        kt, vx = kv_prep(proj, k_norm[i], cos_t, sin_t, q_col0 + qw, q_col0 + qw + kvw, n_kv, hd,
                         _tile(TILE_K, L))
        yc = flash_attention(qr, kt, vx)
        m = gated_merge(ya.reshape(T, dc), zb.reshape(T, dh), yc.reshape(T, qw), gates,
                        bf(w_pa), bf(w_pb), bf(w_pc))
        h = matmul_residual(m, bf(w_o), h, 1.0)
        act = norm_swiglu_up(h, n_ffn2[i], bf(w_ffn2_up))
        h = matmul_residual(act, bf(w_ffn2_down), h, 0.5)
        h = pe_inject(h, n_pe[i], bf(w_pe_gate), pe_all, i, bf(w_pe_proj))

    y_prompt = rms_norm(h, final_norm, 0, bp * L).reshape(bp, L, D)
    y_sample = rms_norm(h, final_norm, bp * L, (B - bp) * L).reshape(B - bp, L, D)
    return (y_prompt, y_sample)
```
